```python
import jax
import jax.numpy as jnp
from jax import lax
import numpy as np

D_MODEL = 1024
BATCH = 8
SEQ = 2048
DEPTH = 2

MEM_LEN = 256
EPS = 1e-6
CONV_W = 4
D_RNN = D_MODEL
RNN_BLOCKS = 16
RNN_BLOCK = D_RNN // RNN_BLOCKS
LRU_C = 8.0
ML_HEADS = 4
ML_DHEAD = D_MODEL // ML_HEADS
D_ML = ML_HEADS * ML_DHEAD
ML_CHUNK = 128
XA_HEADS = 4
XA_DHEAD = D_MODEL // XA_HEADS
D_FF = 3 * D_MODEL
FFN_CONV_W = 3
IN_SPLIT_SIZES = (D_RNN, D_RNN, 2 * D_ML, D_ML, D_ML, 2 * ML_HEADS, D_MODEL, D_MODEL)
D_IN = 2 * D_RNN + 4 * D_ML + 2 * ML_HEADS + 2 * D_MODEL

kernel_name = 'hybrid_rglru_mlstm_xattn_convffn'


def rms_norm(x, g):
    xf = x.astype(jnp.float32)
    y = xf * lax.rsqrt(jnp.mean(xf * xf, axis=-1, keepdims=True) + EPS)
    return (y * g.astype(jnp.float32)).astype(x.dtype)


def causal_dwconv(x, w, b):
    k_w, s = w.shape[0], x.shape[1]
    xp = jnp.pad(x, ((0, 0), (k_w - 1, 0), (0, 0)))
    out = xp[:, 0:s] * w[0] + b
    for j in range(1, k_w):
        out = out + xp[:, j:j + s] * w[j]
    return out


def rg_lru(x, w_a, b_a, w_x, b_x, lam):
    bsz, s, _ = x.shape
    xb = x.reshape(bsz, s, RNN_BLOCKS, RNN_BLOCK)
    r = jax.nn.sigmoid(jnp.einsum('bsgi,gij->bsgj', xb, w_a) + b_a.reshape(RNN_BLOCKS, RNN_BLOCK))
    i = jax.nn.sigmoid(jnp.einsum('bsgi,gij->bsgj', xb, w_x) + b_x.reshape(RNN_BLOCKS, RNN_BLOCK))
    r = r.reshape(bsz, s, D_RNN).astype(jnp.float32)
    i = i.reshape(bsz, s, D_RNN).astype(jnp.float32)
    log_a = -LRU_C * r * jax.nn.softplus(-lam.astype(jnp.float32))
    a = jnp.exp(log_a)
    u = jnp.sqrt(-jnp.expm1(2.0 * log_a)) * (i * x.astype(jnp.float32))

    def combine(left, right):
        a1, b1 = left
        a2, b2 = right
        return a1 * a2, a2 * b1 + b2

    _, h = lax.associative_scan(combine, (a, u), axis=1)
    return h.astype(x.dtype)


def _to_chunks(t, n_chunks):
    bsz, s, h = t.shape[:3]
    t = t.reshape(bsz, n_chunks, s // n_chunks, h, *t.shape[3:])
    return jnp.moveaxis(t, (1, 3), (0, 2))


def mlstm(q, k, v, i_pre, f_pre):
    bsz, s, nh, dh = q.shape
    n_chunks = s // ML_CHUNK
    f32 = jnp.float32
    q = q.astype(f32)
    k = k.astype(f32) * (dh ** -0.5)
    v = v.astype(f32)
    log_i = i_pre.astype(f32)
    log_f = jax.nn.log_sigmoid(f_pre.astype(f32))
    xs = (_to_chunks(q, n_chunks), _to_chunks(k, n_chunks), _to_chunks(v, n_chunks),
          _to_chunks(log_i, n_chunks), _to_chunks(log_f, n_chunks))
    causal = jnp.tril(jnp.ones((ML_CHUNK, ML_CHUNK), dtype=bool))

    def step(carry, inp):
        c_st, n_st, m_st = carry
        qc, kc, vc, li, lf = inp
        b = jnp.cumsum(lf, axis=-1)
        a_inter = b + m_st[..., None]
        d = b[..., :, None] - b[..., None, :] + li[..., None, :]
        d = jnp.where(causal, d, -jnp.inf)
        m_t = jnp.maximum(a_inter, jnp.max(d, axis=-1))
        w_inter = jnp.exp(a_inter - m_t)
        sc = jnp.einsum('bhtd,bhsd->bhts', qc, kc) * jnp.exp(d - m_t[..., None])
        num = (w_inter[..., None] * jnp.einsum('bhtd,bhde->bhte', qc, c_st)
               + jnp.einsum('bhts,bhse->bhte', sc, vc))
        den = w_inter * jnp.einsum('bhtd,bhd->bht', qc, n_st) + jnp.sum(sc, axis=-1)
        h = num / jnp.maximum(jnp.abs(den), jnp.exp(-m_t))[..., None]
        g = b[..., -1]
        u = g[..., None] - b + li
        m_next = jnp.maximum(g + m_st, jnp.max(u, axis=-1))
        decay = jnp.exp(g + m_st - m_next)
        wk = kc * jnp.exp(u - m_next[..., None])[..., None]
        c_next = decay[..., None, None] * c_st + jnp.einsum('bhsd,bhse->bhde', wk, vc)
        n_next = decay[..., None] * n_st + jnp.sum(wk, axis=2)
        return (c_next, n_next, m_next), h

    init = (jnp.zeros((bsz, nh, dh, dh), f32), jnp.zeros((bsz, nh, dh), f32),
            jnp.zeros((bsz, nh), f32))
    _, hs = lax.scan(step, init, xs)
    hs = jnp.moveaxis(hs, (0, 2), (1, 3))
    return hs.reshape(bsz, s, nh * dh)


def head_norm(h, g, n_heads):
    bsz, s, w = h.shape
    hh = h.reshape(bsz, s, n_heads, w // n_heads)
    mu = jnp.mean(hh, axis=-1, keepdims=True)
    var = jnp.mean(jnp.square(hh - mu), axis=-1, keepdims=True)
    hh = (hh - mu) * lax.rsqrt(var + EPS)
    return hh.reshape(bsz, s, w) * g.astype(jnp.float32)


def hybrid_mixer(xn, w_in, rnn_conv_w, rnn_conv_b, lru_wa, lru_ba, lru_wx, lru_bx, lru_lambda,
                 ml_conv_w, ml_conv_b, ml_if_b, ml_norm_g, w_branch_a, w_branch_b, w_mix_out):
    bsz, s, _ = xn.shape
    proj = xn @ w_in
    idx = np.cumsum(IN_SPLIT_SIZES)[:-1].tolist()
    xr, gr, qk, v, o, ifg, ga, gb = jnp.split(proj, idx, axis=-1)
    xr = causal_dwconv(xr, rnn_conv_w, rnn_conv_b)
    ya = jax.nn.gelu(gr) * rg_lru(xr, lru_wa, lru_ba, lru_wx, lru_bx, lru_lambda)
    qk = jax.nn.silu(causal_dwconv(qk, ml_conv_w, ml_conv_b))
    q, k = jnp.split(qk, 2, axis=-1)
    ifg = ifg + ml_if_b
    i_pre, f_pre = jnp.split(ifg, 2, axis=-1)
    shp = (bsz, s, ML_HEADS, ML_DHEAD)
    hm = mlstm(q.reshape(shp), k.reshape(shp), v.reshape(shp), i_pre, f_pre)
    yb = (jax.nn.sigmoid(o.astype(jnp.float32)) * head_norm(hm, ml_norm_g, ML_HEADS)).astype(xn.dtype)
    y = jax.nn.sigmoid(ga) * (ya @ w_branch_a) + jax.nn.sigmoid(gb) * (yb @ w_branch_b)
    return y @ w_mix_out


def cross_attention(xn, memn, w_q, w_kv, w_o):
    bsz, s, _ = xn.shape
    m = memn.shape[1]
    q = (xn @ w_q).reshape(bsz, s, XA_HEADS, XA_DHEAD)
    k, v = jnp.split(memn @ w_kv, 2, axis=-1)
    k = k.reshape(bsz, m, XA_HEADS, XA_DHEAD)
    v = v.reshape(bsz, m, XA_HEADS, XA_DHEAD)
    sc = jnp.einsum('bshd,bmhd->bhsm', q, k).astype(jnp.float32) * (XA_DHEAD ** -0.5)
    p = jax.nn.softmax(sc, axis=-1).astype(v.dtype)
    out = jnp.einsum('bhsm,bmhd->bshd', p, v).reshape(bsz, s, XA_HEADS * XA_DHEAD)
    return out @ w_o


def conv_ffn(xn, w_up, conv_w, conv_b, w_down):
    h = causal_dwconv(xn @ w_up, conv_w, conv_b)
    g, u = jnp.split(h, 2, axis=-1)
    return (jax.nn.gelu(g) * u) @ w_down


def setup_inputs(seed: int = 0) -> dict:
    key = jax.random.key(seed)
    ks = iter(jax.random.split(key, 40))
    f32 = jnp.float32

    def nrm(shape, scale):
        return jax.random.normal(next(ks), shape, f32) * scale

    def gain(shape):
        return 1.0 + nrm(shape, 0.02)

    x = nrm((BATCH, SEQ, D_MODEL), 1.0)
    mem = nrm((BATCH, MEM_LEN, D_MODEL), 1.0)
    norm_mix_g = gain((DEPTH, D_MODEL))
    w_in = nrm((DEPTH, D_MODEL, D_IN), D_MODEL ** -0.5)
    rnn_conv_w = nrm((DEPTH, CONV_W, D_RNN), CONV_W ** -0.5)
    rnn_conv_b = nrm((DEPTH, D_RNN), 0.01)
    lru_wa = nrm((DEPTH, RNN_BLOCKS, RNN_BLOCK, RNN_BLOCK), RNN_BLOCK ** -0.5)
    lru_ba = nrm((DEPTH, D_RNN), 0.01)
    lru_wx = nrm((DEPTH, RNN_BLOCKS, RNN_BLOCK, RNN_BLOCK), RNN_BLOCK ** -0.5)
    lru_bx = nrm((DEPTH, D_RNN), 0.01)
    a_c = jax.random.uniform(next(ks), (DEPTH, D_RNN), f32, 0.9, 0.999)
    a0 = a_c ** (1.0 / LRU_C)
    lru_lambda = jnp.log(a0) - jnp.log1p(-a0)
    ml_conv_w = nrm((DEPTH, CONV_W, 2 * D_ML), CONV_W ** -0.5)
    ml_conv_b = nrm((DEPTH, 2 * D_ML), 0.01)
    i_bias = nrm((DEPTH, ML_HEADS), 0.1)
    f_bias = jnp.linspace(3.0, 6.0, ML_HEADS, dtype=f32)[None, :] + nrm((DEPTH, ML_HEADS), 0.1)
    ml_if_b = jnp.concatenate([i_bias, f_bias], axis=-1)
    ml_norm_g = gain((DEPTH, D_ML))
    w_branch_a = nrm((DEPTH, D_RNN, D_MODEL), D_RNN ** -0.5)
    w_branch_b = nrm((DEPTH, D_ML, D_MODEL), D_ML ** -0.5)
    w_mix_out = nrm((DEPTH, D_MODEL, D_MODEL), D_MODEL ** -0.5)
    norm_xa_g = gain((DEPTH, D_MODEL))
    xa_wq = nrm((DEPTH, D_MODEL, XA_HEADS * XA_DHEAD), D_MODEL ** -0.5)
    xa_wkv = nrm((DEPTH, D_MODEL, 2 * XA_HEADS * XA_DHEAD), D_MODEL ** -0.5)
    xa_wo = nrm((DEPTH, XA_HEADS * XA_DHEAD, D_MODEL), (XA_HEADS * XA_DHEAD) ** -0.5)
    norm_ffn_g = gain((DEPTH, D_MODEL))
    ffn_w_up = nrm((DEPTH, D_MODEL, 2 * D_FF), D_MODEL ** -0.5)
    ffn_conv_w = nrm((DEPTH, FFN_CONV_W, 2 * D_FF), FFN_CONV_W ** -0.5)
    ffn_conv_b = nrm((DEPTH, 2 * D_FF), 0.01)
    ffn_w_down = nrm((DEPTH, D_FF, D_MODEL), D_FF ** -0.5)
    mem_norm_g = gain((D_MODEL,))
    final_norm_g = gain((D_MODEL,))
    return {'x': x, 'mem': mem, 'norm_mix_g': norm_mix_g, 'w_in': w_in,
            'rnn_conv_w': rnn_conv_w, 'rnn_conv_b': rnn_conv_b,
            'lru_wa': lru_wa, 'lru_ba': lru_ba, 'lru_wx': lru_wx, 'lru_bx': lru_bx,
            'lru_lambda': lru_lambda, 'ml_conv_w': ml_conv_w, 'ml_conv_b': ml_conv_b,
            'ml_if_b': ml_if_b, 'ml_norm_g': ml_norm_g, 'w_branch_a': w_branch_a,
            'w_branch_b': w_branch_b, 'w_mix_out': w_mix_out, 'norm_xa_g': norm_xa_g,
            'xa_wq': xa_wq, 'xa_wkv': xa_wkv, 'xa_wo': xa_wo, 'norm_ffn_g': norm_ffn_g,
            'ffn_w_up': ffn_w_up, 'ffn_conv_w': ffn_conv_w, 'ffn_conv_b': ffn_conv_b,
            'ffn_w_down': ffn_w_down, 'mem_norm_g': mem_norm_g, 'final_norm_g': final_norm_g}


def reference(x, mem, norm_mix_g, w_in, rnn_conv_w, rnn_conv_b, lru_wa, lru_ba, lru_wx, lru_bx,
              lru_lambda, ml_conv_w, ml_conv_b, ml_if_b, ml_norm_g, w_branch_a, w_branch_b,
              w_mix_out, norm_xa_g, xa_wq, xa_wkv, xa_wo, norm_ffn_g, ffn_w_up, ffn_conv_w,
              ffn_conv_b, ffn_w_down, mem_norm_g, final_norm_g):
    memn = rms_norm(mem, mem_norm_g)
    h = x
    for l in range(DEPTH):
        h = h + hybrid_mixer(rms_norm(h, norm_mix_g[l]), w_in[l], rnn_conv_w[l], rnn_conv_b[l],
                             lru_wa[l], lru_ba[l], lru_wx[l], lru_bx[l], lru_lambda[l],
                             ml_conv_w[l], ml_conv_b[l], ml_if_b[l], ml_norm_g[l],
                             w_branch_a[l], w_branch_b[l], w_mix_out[l])
        h = h + cross_attention(rms_norm(h, norm_xa_g[l]), memn, xa_wq[l], xa_wkv[l], xa_wo[l])
        h = h + conv_ffn(rms_norm(h, norm_ffn_g[l]), ffn_w_up[l], ffn_conv_w[l], ffn_conv_b[l],
                         ffn_w_down[l])
    return rms_norm(h, final_norm_g)
```

```python
import functools

import jax
import jax.numpy as jnp
from jax import lax
from jax.experimental import pallas as pl
from jax.experimental.pallas import tpu as pltpu

F32 = jnp.float32
BF16 = jnp.bfloat16

EPS = 1e-6
LRU_C = 8.0
CONV_W = 4
FFN_CONV_W = 3
RNN_BLOCK = 64
ML_HEADS = 4
ML_CHUNK = 128
XA_HEADS = 4

V7X_SUBLANES = 8
V7X_LANES = 128
V7X_MXU_DIM = 256
V7X_VMEM_BYTES = 64 * 1024 * 1024
V7X_VMEM_REQUEST_CAP = 56 * 1024 * 1024

SEG_GR, SEG_XR, SEG_Q, SEG_K, SEG_V, SEG_O, SEG_GA, SEG_GB = range(8)
SLOT_YA, SLOT_Q, SLOT_K, SLOT_V, SLOT_O, SLOT_GA, SLOT_GB = range(7)
N_SEG = 8
N_SLOT = 7
IFG_ROWS = 16


def _vmem_limit(n_bytes):
    return int(min(V7X_VMEM_REQUEST_CAP, n_bytes + n_bytes // 2 + (4 << 20)))


def _params(n_grid, vmem_bytes):
    return pltpu.CompilerParams(
        dimension_semantics=("arbitrary",) * n_grid,
        vmem_limit_bytes=_vmem_limit(vmem_bytes),
    )


def _rms_norm(xf, g):
    ms = jnp.mean(xf * xf, axis=-1, keepdims=True)
    return (xf * lax.rsqrt(ms + EPS)) * g


def _softplus(z):
    return jnp.maximum(z, 0.0) + jnp.log1p(jnp.exp(-jnp.abs(z)))


def _row_chunks(n_rows, chunk, body):
    assert n_rows % chunk == 0

    def step(r, carry):
        body(pl.multiple_of(r * chunk, chunk))
        return carry

    lax.fori_loop(0, n_rows // chunk, step, 0)


def _memkv_kernel(mem_ref, g_ref, w_ref, o_ref):
    memn = _rms_norm(mem_ref[0], g_ref[...]).astype(BF16)
    o_ref[0, 0] = jnp.dot(memn, w_ref[0], preferred_element_type=F32).astype(BF16)


def _memkv(mem, mem_g, wkv):
    depth, d, n = wkv.shape
    b, m, _ = mem.shape
    vmem = 2 * (m * d * 4 + d * n * 2 + m * n * 2) + m * n * 4
    return pl.pallas_call(
        _memkv_kernel,
        grid=(depth, b),
        in_specs=[
            pl.BlockSpec((1, m, d), lambda l, i: (i, 0, 0)),
            pl.BlockSpec((1, d), lambda l, i: (0, 0)),
            pl.BlockSpec((1, d, n), lambda l, i: (l, 0, 0)),
        ],
        out_specs=pl.BlockSpec((1, 1, m, n), lambda l, i: (l, i, 0, 0)),
        out_shape=jax.ShapeDtypeStruct((depth, b, m, n), BF16),
        compiler_params=_params(2, vmem),
        name="memkv",
    )(mem, mem_g, wkv)


def _inproj_kernel(x_ref, g_ref, w_ref, wif_ref, bif_ref, cw_ref, cb_ref, wg_ref, bg_ref,
                   lam_ref, seg_ref, ifg_ref,
                   xn_scr, pc_scr, gelu_scr, xc_scr, xcb_scr, gt_scr, a_scr, u_scr,
                   carry_scr, hlast_scr, *, tm, rc):
    i = pl.program_id(1)
    j = pl.program_id(2)
    c = xn_scr.shape[1]
    halo = V7X_SUBLANES

    @pl.when(j == 0)
    def _():
        xn = _rms_norm(x_ref[0], g_ref[0]).astype(BF16)
        xn_scr[...] = xn
        ifg = lax.dot_general(wif_ref[0], xn, (((1,), (1,)), ((), ())),
                              preferred_element_type=F32)
        ifg_ref[0] = ifg + bif_ref[0]

    @pl.when((i == 0) & (j == 0))
    def _():
        carry_scr[...] = jnp.zeros_like(carry_scr)
        hlast_scr[...] = jnp.zeros_like(hlast_scr)

    pc_scr[halo:halo + tm, :] = jnp.dot(xn_scr[...], w_ref[0], preferred_element_type=F32)

    def conv_rows(ci, r0, n):
        blk = pc_scr[pl.ds(r0, n + halo), :]
        cw = cw_ref[0, ci]
        out = cb_ref[0, ci] + blk[halo:halo + n] * cw[3:4]
        for k in range(CONV_W - 1):
            off = halo - (CONV_W - 1) + k
            out = out + blk[off:off + n] * cw[k:k + 1]
        return out

    def load_carry(ci):
        pc_scr[0:halo, :] = carry_scr[ci]

    def save_carry(ci):
        carry_scr[ci] = pc_scr[tm:tm + halo, :]

    @pl.when(j == SEG_GR)
    def _():
        def body(r0):
            gelu_scr[pl.ds(r0, rc), :] = jax.nn.gelu(pc_scr[pl.ds(r0 + halo, rc), :])
        _row_chunks(tm, rc, body)

    @pl.when(j == SEG_XR)
    def _():
        load_carry(0)

        def conv_body(r0):
            xc = conv_rows(0, r0, rc)
            xc_scr[pl.ds(r0, rc), :] = xc
            xcb_scr[pl.ds(r0, rc), :] = xc.astype(BF16)
        _row_chunks(tm, rc, conv_body)
        save_carry(0)

        c_row = -LRU_C * _softplus(-lam_ref[0])
        gw = V7X_MXU_DIM
        for g in range(c // gw):
            sl = slice(g * gw, (g + 1) * gw)
            gt_scr[...] = jnp.dot(xcb_scr[:, sl], wg_ref[0, g], preferred_element_type=F32)
            c_g = c_row[:, sl]
            ba_g = bg_ref[0, 0][:, sl]
            bx_g = bg_ref[0, 1][:, sl]

            def gate_body(r0, sl=sl, c_g=c_g, ba_g=ba_g, bx_g=bx_g):
                gt = gt_scr[pl.ds(r0, rc), :]
                r = jax.nn.sigmoid(gt[:, :gw] + ba_g)
                ig = jax.nn.sigmoid(gt[:, gw:] + bx_g)
                log_a = c_g * r
                a = jnp.exp(log_a)
                mult = jnp.sqrt(-jnp.tanh(log_a) * (a * a + 1.0))
                a_scr[pl.ds(r0, rc), sl] = a
                u_scr[pl.ds(r0, rc), sl] = mult * (ig * xc_scr[pl.ds(r0, rc), sl])
            _row_chunks(tm, rc, gate_body)

        row = lax.broadcasted_iota(jnp.int32, (halo, c), 0)

        def scan_body(r, hprev):
            r0 = pl.multiple_of(r * halo, halo)
            a = a_scr[pl.ds(r0, halo), :]
            u = u_scr[pl.ds(r0, halo), :]
            for k in (1, 2, 4):
                keep = row >= k
                u = jnp.where(keep, u + a * pltpu.roll(u, k, 0), u)
                a = jnp.where(keep, a * pltpu.roll(a, k, 0), a)
            h = u + a * hprev
            u_scr[pl.ds(r0, halo), :] = h
            return jnp.broadcast_to(h[halo - 1:halo, :], (halo, c))

        hlast_scr[...] = lax.fori_loop(0, tm // halo, scan_body, hlast_scr[...])

        def out_body(r0):
            ya = gelu_scr[pl.ds(r0, rc), :] * u_scr[pl.ds(r0, rc), :]
            seg_ref[0, 0, pl.ds(r0, rc), :] = ya.astype(BF16)
        _row_chunks(tm, rc, out_body)

    def conv_silu_segment(ci):
        load_carry(ci)

        def body(r0):
            seg_ref[0, 0, pl.ds(r0, rc), :] = jax.nn.silu(conv_rows(ci, r0, rc)).astype(BF16)
        _row_chunks(tm, rc, body)
        save_carry(ci)

    @pl.when(j == SEG_Q)
    def _():
        conv_silu_segment(1)

    @pl.when(j == SEG_K)
    def _():
        conv_silu_segment(2)

    @pl.when(j >= SEG_V)
    def _():
        def body(r0):
            seg_ref[0, 0, pl.ds(r0, rc), :] = pc_scr[pl.ds(r0 + halo, rc), :].astype(BF16)
        _row_chunks(tm, rc, body)


def _inproj(layer, h, norm_g, w_main, w_ifT, b_if, conv_w, conv_b, w_gate, b_gate, lam, *, tm=512,
            rc=32):
    b, s, d = h.shape
    n_groups = d // V7X_MXU_DIM
    halo = V7X_SUBLANES
    kern = functools.partial(_inproj_kernel, tm=tm, rc=rc)
    vmem = (2 * tm * d * 4 + 2 * d * d * 2 + 2 * tm * d * 2 + 2 * IFG_ROWS * tm * 4
            + tm * d * 2 + (tm + halo) * d * 4 + 4 * tm * d * 4 + tm * d * 2
            + tm * 2 * V7X_MXU_DIM * 4 + 2 * n_groups * V7X_MXU_DIM * 2 * V7X_MXU_DIM * 2
            + 2 * tm * d * 4)
    l = layer
    return pl.pallas_call(
        kern,
        grid=(b, s // tm, N_SEG),
        in_specs=[
            pl.BlockSpec((1, tm, d), lambda bi, i, j: (bi, i, 0)),
            pl.BlockSpec((1, 1, d), lambda bi, i, j: (l, 0, 0)),
            pl.BlockSpec((1, d, d), lambda bi, i, j: (l, 0, j)),
            pl.BlockSpec((1, IFG_ROWS, d), lambda bi, i, j: (l, 0, 0)),
            pl.BlockSpec((1, IFG_ROWS, 1), lambda bi, i, j: (l, 0, 0)),
            pl.BlockSpec((1, 3, CONV_W, d), lambda bi, i, j: (l, 0, 0, 0)),
            pl.BlockSpec((1, 3, 1, d), lambda bi, i, j: (l, 0, 0, 0)),
            pl.BlockSpec((1, n_groups, V7X_MXU_DIM, 2 * V7X_MXU_DIM), lambda bi, i, j: (l, 0, 0, 0)),
            pl.BlockSpec((1, 2, 1, d), lambda bi, i, j: (l, 0, 0, 0)),
            pl.BlockSpec((1, 1, d), lambda bi, i, j: (l, 0, 0)),
        ],
        out_specs=[
            pl.BlockSpec((1, 1, tm, d), lambda bi, i, j: (jnp.maximum(j - 1, 0), bi, i, 0)),
            pl.BlockSpec((1, IFG_ROWS, tm), lambda bi, i, j: (bi, 0, i)),
        ],
        out_shape=[
            jax.ShapeDtypeStruct((N_SLOT, b, s, d), BF16),
            jax.ShapeDtypeStruct((b, IFG_ROWS, s), F32),
        ],
        scratch_shapes=[
            pltpu.VMEM((tm, d), BF16),
            pltpu.VMEM((tm + halo, d), F32),
            pltpu.VMEM((tm, d), F32),
            pltpu.VMEM((tm, d), F32),
            pltpu.VMEM((tm, d), BF16),
            pltpu.VMEM((tm, 2 * V7X_MXU_DIM), F32),
            pltpu.VMEM((tm, d), F32),
            pltpu.VMEM((tm, d), F32),
            pltpu.VMEM((3, halo, d), F32),
            pltpu.VMEM((halo, d), F32),
        ],
        compiler_params=_params(3, vmem),
        name=f"inproj{layer}",
    )(h, norm_g, w_main, w_ifT, b_if, conv_w, conv_b, w_gate, b_gate, lam)


def _lane_scan(x, lane, op, fill):
    n = x.shape[-1]
    k = 1
    while k < n:
        x = op(x, jnp.where(lane >= k, pltpu.roll(x, k, 1), fill))
        k *= 2
    return x


def _mlstm_kernel(q_ref, k_ref, v_ref, o_ref, ifg_ref, ng_ref, y_ref, c_scr, n_scr, m_scr):
    ci = pl.program_id(1)
    L = ML_CHUNK
    dh = c_scr.shape[1]
    rows = V7X_SUBLANES

    @pl.when(ci == 0)
    def _():
        c_scr[...] = jnp.zeros_like(c_scr)
        n_scr[...] = jnp.zeros_like(n_scr)
        m_scr[...] = jnp.zeros_like(m_scr)

    ifg = ifg_ref[0]
    li = ifg[0:rows]
    lf = -_softplus(-ifg[rows:2 * rows])
    lane = lax.broadcasted_iota(jnp.int32, (rows, L), 1)
    bcum = _lane_scan(lf, lane, jnp.add, 0.0)
    m_st = m_scr[...]
    g_tot = jnp.broadcast_to(bcum[:, L - 1:L], (rows, L))
    a_inter = bcum + m_st
    r1 = li - bcum
    m_t = jnp.maximum(a_inter, bcum + _lane_scan(r1, lane, jnp.maximum, -jnp.inf))
    c1 = bcum - m_t
    w_inter = jnp.exp(a_inter - m_t)
    e_negm = jnp.exp(-m_t)
    u = g_tot - bcum + li
    m_next = jnp.maximum(g_tot + m_st, jnp.max(u, axis=-1, keepdims=True))
    decay = jnp.exp(g_tot + m_st - m_next)
    uexp = jnp.exp(u - m_next)
    m_scr[...] = m_next

    stacked = jnp.concatenate(
        [c1, w_inter, e_negm, uexp, jnp.zeros((L - 4 * rows, L), F32)], axis=0)
    cols = stacked.T

    tt = lax.broadcasted_iota(jnp.int32, (L, L), 0)
    ss = lax.broadcasted_iota(jnp.int32, (L, L), 1)
    causal = ss <= tt
    k_scale = dh ** -0.5

    for hd in range(ML_HEADS):
        sl = slice(hd * dh, (hd + 1) * dh)
        qh = q_ref[0, 0, :, sl]
        kh = k_ref[0, 0, :, sl] * jnp.asarray(k_scale, BF16)
        vh = v_ref[0, 0, :, sl]
        c1c = cols[:, hd:hd + 1]
        wic = cols[:, rows + hd:rows + hd + 1]
        enc = cols[:, 2 * rows + hd:2 * rows + hd + 1]
        uec = cols[:, 3 * rows + hd:3 * rows + hd + 1]
        r1r = r1[hd:hd + 1, :]
        dec = decay[hd:hd + 1, 0:1]

        p = jnp.exp(jnp.where(causal, c1c + r1r, -jnp.inf))
        sc = lax.dot_general(qh, kh, (((1,), (1,)), ((), ())),
                             preferred_element_type=F32) * p
        c_old = c_scr[hd]
        n_old = n_scr[hd:hd + 1, :]
        num = wic * jnp.dot(qh, c_old.astype(BF16), preferred_element_type=F32) \
            + jnp.dot(sc.astype(BF16), vh, preferred_element_type=F32)
        qn = jnp.sum(qh.astype(F32) * n_old, axis=-1, keepdims=True)
        den = wic * qn + jnp.sum(sc, axis=-1, keepdims=True)
        hh = num * (1.0 / jnp.maximum(jnp.abs(den), enc))

        wk = kh.astype(F32) * uec
        c_scr[hd] = dec * c_old + lax.dot_general(
            wk.astype(BF16), vh, (((0,), (0,)), ((), ())), preferred_element_type=F32)
        n_scr[hd:hd + 1, :] = dec * n_old + jnp.sum(wk, axis=0, keepdims=True)

        mu = jnp.mean(hh, axis=-1, keepdims=True)
        cen = hh - mu
        var = jnp.mean(cen * cen, axis=-1, keepdims=True)
        hn = (cen * lax.rsqrt(var + EPS)) * ng_ref[0][:, sl]
        gate = jax.nn.sigmoid(o_ref[0, 0, :, sl].astype(F32))
        y_ref[0, :, sl] = (gate * hn).astype(BF16)


def _mlstm(layer, segs, ifg, ml_norm_g):
    _, b, s, d = segs.shape
    L = ML_CHUNK
    dh = d // ML_HEADS
    l = layer

    def seg_spec(slot):
        return pl.BlockSpec((1, 1, L, d), lambda bi, ci: (slot, bi, ci, 0))

    vmem = 2 * (4 * L * d * 2 + IFG_ROWS * L * 4 + L * d * 2) + ML_HEADS * dh * dh * 4 \
        + 16 * L * d * 4
    return pl.pallas_call(
        _mlstm_kernel,
        grid=(b, s // L),
        in_specs=[
            seg_spec(SLOT_Q), seg_spec(SLOT_K), seg_spec(SLOT_V), seg_spec(SLOT_O),
            pl.BlockSpec((1, IFG_ROWS, L), lambda bi, ci: (bi, 0, ci)),
            pl.BlockSpec((1, 1, d), lambda bi, ci: (l, 0, 0)),
        ],
        out_specs=pl.BlockSpec((1, L, d), lambda bi, ci: (bi, ci, 0)),
        out_shape=jax.ShapeDtypeStruct((b, s, d), BF16),
        scratch_shapes=[
            pltpu.VMEM((ML_HEADS, dh, dh), F32),
            pltpu.VMEM((V7X_SUBLANES, dh), F32),
            pltpu.VMEM((V7X_SUBLANES, L), F32),
        ],
        compiler_params=_params(2, vmem),
        name=f"mlstm{layer}",
    )(segs, segs, segs, segs, ifg, ml_norm_g)


def _merge_kernel(ya_ref, yb_ref, ga_ref, gb_ref, h_ref, wa_ref, wb_ref, wm_ref, o_ref):
    pa = jnp.dot(ya_ref[0, 0], wa_ref[0], preferred_element_type=F32)
    pb = jnp.dot(yb_ref[0], wb_ref[0], preferred_element_type=F32)
    y = jax.nn.sigmoid(ga_ref[0, 0].astype(F32)) * pa + jax.nn.sigmoid(gb_ref[0, 0].astype(F32)) * pb
    o_ref[0] = h_ref[0] + jnp.dot(y.astype(BF16), wm_ref[0], preferred_element_type=F32)


def _merge(layer, segs, yb, h, wa, wb, wm, *, tm=512):
    b, s, d = h.shape
    l = layer

    def seg_spec(slot):
        return pl.BlockSpec((1, 1, tm, d), lambda bi, i: (slot, bi, i, 0))

    w_spec = pl.BlockSpec((1, d, d), lambda bi, i: (l, 0, 0))
    row_spec = pl.BlockSpec((1, tm, d), lambda bi, i: (bi, i, 0))
    vmem = 2 * (4 * tm * d * 2 + 2 * tm * d * 4 + 3 * d * d * 2) + 4 * tm * d * 4
    return pl.pallas_call(
        _merge_kernel,
        grid=(b, s // tm),
        in_specs=[seg_spec(SLOT_YA), row_spec, seg_spec(SLOT_GA), seg_spec(SLOT_GB), row_spec,
                  w_spec, w_spec, w_spec],
        out_specs=row_spec,
        out_shape=jax.ShapeDtypeStruct((b, s, d), F32),
        compiler_params=_params(2, vmem),
        name=f"merge{layer}",
    )(segs, yb, segs, segs, h, wa, wb, wm)


def _xattn_kernel(h_ref, g_ref, wq_ref, kv_ref, wo_ref, o_ref, att_scr):
    d = h_ref.shape[2]
    dh = d // XA_HEADS
    hres = h_ref[0]
    xn = _rms_norm(hres, g_ref[0]).astype(BF16)
    q = jnp.dot(xn, wq_ref[0], preferred_element_type=F32).astype(BF16)
    scale = dh ** -0.5
    for hd in range(XA_HEADS):
        sl = slice(hd * dh, (hd + 1) * dh)
        kh = kv_ref[0, 0, :, sl]
        vh = kv_ref[0, 0, :, d + hd * dh:d + (hd + 1) * dh]
        sc = lax.dot_general(q[:, sl], kh, (((1,), (1,)), ((), ())),
                             preferred_element_type=F32) * scale
        e = jnp.exp(sc - jnp.max(sc, axis=-1, keepdims=True))
        p = e * (1.0 / jnp.sum(e, axis=-1, keepdims=True))
        att_scr[:, sl] = jnp.dot(p.astype(BF16), vh, preferred_element_type=F32).astype(BF16)
    o_ref[0] = hres + jnp.dot(att_scr[...], wo_ref[0], preferred_element_type=F32)


def _xattn(layer, h, norm_g, wq, kv, wo, *, tm=512):
    b, s, d = h.shape
    m = kv.shape[2]
    l = layer
    row_spec = pl.BlockSpec((1, tm, d), lambda bi, i: (bi, i, 0))
    w_spec = pl.BlockSpec((1, d, d), lambda bi, i: (l, 0, 0))
    vmem = 2 * (2 * tm * d * 4 + 2 * d * d * 2 + m * 2 * d * 2) + tm * d * 2 + 6 * tm * d * 4
    return pl.pallas_call(
        _xattn_kernel,
        grid=(b, s // tm),
        in_specs=[
            row_spec,
            pl.BlockSpec((1, 1, d), lambda bi, i: (l, 0, 0)),
            w_spec,
            pl.BlockSpec((1, 1, m, 2 * d), lambda bi, i: (l, bi, 0, 0)),
            w_spec,
        ],
        out_specs=row_spec,
        out_shape=jax.ShapeDtypeStruct((b, s, d), F32),
        scratch_shapes=[pltpu.VMEM((tm, d), BF16)],
        compiler_params=_params(2, vmem),
        name=f"xattn{layer}",
    )(h, norm_g, wq, kv, wo)


def _ffn_kernel(x_ref, g_ref, wg_ref, wu_ref, cwg_ref, cwu_ref, cbg_ref, cbu_ref, wd_ref, fg_ref,
                o_ref, xn_scr, acc_scr, pc_scr, act_scr, carry_scr, *, tm, rc, n_j, final_norm):
    i = pl.program_id(1)
    j = pl.program_id(2)
    halo = V7X_SUBLANES

    @pl.when(j == 0)
    def _():
        xn_scr[...] = _rms_norm(x_ref[0], g_ref[0]).astype(BF16)
        acc_scr[...] = jnp.zeros_like(acc_scr)

    @pl.when(i == 0)
    def _():
        carry_scr[j] = jnp.zeros(carry_scr.shape[1:], F32)

    xn = xn_scr[...]
    for half, w_ref in enumerate((wg_ref, wu_ref)):
        pc_scr[half, halo:halo + tm, :] = jnp.dot(xn, w_ref[0], preferred_element_type=F32)
        pc_scr[half, 0:halo, :] = carry_scr[j, half]

    def conv_rows(half, cw_ref, cb_ref, r0):
        blk = pc_scr[half, pl.ds(r0, rc + halo), :]
        cw = cw_ref[0]
        out = cb_ref[0] + blk[halo:halo + rc] * cw[2:3]
        for k in range(FFN_CONV_W - 1):
            off = halo - (FFN_CONV_W - 1) + k
            out = out + blk[off:off + rc] * cw[k:k + 1]
        return out

    def body(r0):
        gpre = conv_rows(0, cwg_ref, cbg_ref, r0)
        upre = conv_rows(1, cwu_ref, cbu_ref, r0)
        act_scr[pl.ds(r0, rc), :] = (jax.nn.gelu(gpre) * upre).astype(BF16)
    _row_chunks(tm, rc, body)

    for half in range(2):
        carry_scr[j, half] = pc_scr[half, tm:tm + halo, :]

    acc_scr[...] += jnp.dot(act_scr[...], wd_ref[0], preferred_element_type=F32)

    @pl.when(j == n_j - 1)
    def _():
        def out_body(r0):
            res = x_ref[0, pl.ds(r0, rc), :] + acc_scr[pl.ds(r0, rc), :]
            if final_norm:
                res = _rms_norm(res, fg_ref[...])
            o_ref[0, pl.ds(r0, rc), :] = res
        _row_chunks(tm, rc, out_body)


def _ffn(layer, h, norm_g, w_up, conv_w, conv_b, w_down, final_g, *, final_norm, tm=1024, tf=512,
         rc=32):
    b, s, d = h.shape
    d_ff = w_down.shape[1]
    n_j = d_ff // tf
    halo = V7X_SUBLANES
    l = layer
    kern = functools.partial(_ffn_kernel, tm=tm, rc=rc, n_j=n_j, final_norm=final_norm)
    row_spec = pl.BlockSpec((1, tm, d), lambda bi, i, j: (bi, i, 0))
    vmem = (4 * tm * d * 4 + 2 * (2 * d * tf * 2 + tf * d * 2) + tm * d * 2 + tm * d * 4
            + 2 * (tm + halo) * tf * 4 + tm * tf * 2 + n_j * 2 * halo * tf * 4 + 2 * tm * tf * 4)
    return pl.pallas_call(
        kern,
        grid=(b, s // tm, n_j),
        in_specs=[
            row_spec,
            pl.BlockSpec((1, 1, d), lambda bi, i, j: (l, 0, 0)),
            pl.BlockSpec((1, d, tf), lambda bi, i, j: (l, 0, j)),
            pl.BlockSpec((1, d, tf), lambda bi, i, j: (l, 0, n_j + j)),
            pl.BlockSpec((1, FFN_CONV_W, tf), lambda bi, i, j: (l, 0, j)),
            pl.BlockSpec((1, FFN_CONV_W, tf), lambda bi, i, j: (l, 0, n_j + j)),
            pl.BlockSpec((1, 1, tf), lambda bi, i, j: (l, 0, j)),
            pl.BlockSpec((1, 1, tf), lambda bi, i, j: (l, 0, n_j + j)),
            pl.BlockSpec((1, tf, d), lambda bi, i, j: (l, j, 0)),
            pl.BlockSpec((1, d), lambda bi, i, j: (0, 0)),
        ],
        out_specs=row_spec,
        out_shape=jax.ShapeDtypeStruct((b, s, d), F32),
        scratch_shapes=[
            pltpu.VMEM((tm, d), BF16),
            pltpu.VMEM((tm, d), F32),
            pltpu.VMEM((2, tm + halo, tf), F32),
            pltpu.VMEM((tm, tf), BF16),
            pltpu.VMEM((n_j, 2, halo, tf), F32),
        ],
        compiler_params=_params(3, vmem),
        name=f"ffn{layer}",
    )(h, norm_g, w_up, w_up, conv_w, conv_w, conv_b, conv_b, w_down, final_g)


def _block_diag_groups(w):
    depth, n_blocks, bs, _ = w.shape
    per = V7X_MXU_DIM // bs
    wg = w.reshape(depth, n_blocks // per, per, bs, bs)
    eye = jnp.eye(per, dtype=w.dtype)
    out = wg[:, :, :, :, None, :] * eye[None, None, :, None, :, None]
    return out.reshape(depth, n_blocks // per, per * bs, per * bs)


def kernel(x, mem, norm_mix_g, w_in, rnn_conv_w, rnn_conv_b, lru_wa, lru_ba, lru_wx, lru_bx, lru_lambda, ml_conv_w, ml_conv_b, ml_if_b, ml_norm_g, w_branch_a, w_branch_b, w_mix_out, norm_xa_g, xa_wq, xa_wkv, xa_wo, norm_ffn_g, ffn_w_up, ffn_conv_w, ffn_conv_b, ffn_w_down, mem_norm_g, final_norm_g):
    depth, d, _ = w_in.shape
    d_ml = ml_norm_g.shape[1]
    assert d == d_ml == lru_lambda.shape[1], "kernels assume D_RNN == D_ML == D_MODEL"
    assert ML_HEADS <= V7X_SUBLANES and x.shape[1] % ML_CHUNK == 0

    o_xr, o_gr, o_q, o_k, o_v, o_o = (n * d for n in range(6))
    o_if = 6 * d
    o_ga = o_if + 2 * ML_HEADS
    o_gb = o_ga + d

    def cols(o):
        return w_in[:, :, o:o + d]

    w_main = jnp.concatenate(
        [cols(o_gr), cols(o_xr), cols(o_q), cols(o_k), cols(o_v), cols(o_o), cols(o_ga), cols(o_gb)],
        axis=-1).astype(BF16)
    w_if = jnp.swapaxes(w_in[:, :, o_if:o_ga], 1, 2)
    pad = jnp.zeros((depth, V7X_SUBLANES - ML_HEADS, d), w_in.dtype)
    w_ifT = jnp.concatenate([w_if[:, :ML_HEADS], pad, w_if[:, ML_HEADS:], pad], axis=1).astype(BF16)
    bpad = jnp.zeros((depth, V7X_SUBLANES - ML_HEADS), ml_if_b.dtype)
    b_if = jnp.concatenate([ml_if_b[:, :ML_HEADS], bpad, ml_if_b[:, ML_HEADS:], bpad], axis=1)[..., None]

    conv_w = jnp.stack([rnn_conv_w, ml_conv_w[:, :, :d], ml_conv_w[:, :, d:]], axis=1)
    conv_b = jnp.stack([rnn_conv_b, ml_conv_b[:, :d], ml_conv_b[:, d:]], axis=1)[:, :, None, :]
    w_gate = jnp.concatenate([_block_diag_groups(lru_wa), _block_diag_groups(lru_wx)], axis=-1).astype(BF16)
    b_gate = jnp.stack([lru_ba, lru_bx], axis=1)[:, :, None, :]

    def row(g):
        return g[:, None, :]

    wa, wb, wm = (w.astype(BF16) for w in (w_branch_a, w_branch_b, w_mix_out))
    wq, wkv, wo = (w.astype(BF16) for w in (xa_wq, xa_wkv, xa_wo))
    w_up, w_down = ffn_w_up.astype(BF16), ffn_w_down.astype(BF16)
    ffn_cb = row(ffn_conv_b)

    kv = _memkv(mem, mem_norm_g[None, :], wkv)
    h = x
    for l in range(depth):
        segs, ifg = _inproj(l, h, row(norm_mix_g), w_main, w_ifT, b_if, conv_w, conv_b, w_gate,
                            b_gate, row(lru_lambda))
        yb = _mlstm(l, segs, ifg, row(ml_norm_g))
        h = _merge(l, segs, yb, h, wa, wb, wm)
        h = _xattn(l, h, row(norm_xa_g), wq, kv, wo)
        h = _ffn(l, h, row(norm_ffn_g), w_up, ffn_conv_w, ffn_cb, w_down, final_norm_g[None, :],
                 final_norm=(l == depth - 1))
    return h
```

```python
import functools

import jax
import jax.numpy as jnp
from jax import lax
from jax.experimental import pallas as pl
from jax.experimental.pallas import tpu as pltpu

F32 = jnp.float32
BF16 = jnp.bfloat16

EPS = 1e-6
LRU_C = 8.0
CONV_W = 4
FFN_CONV_W = 3
RNN_BLOCK = 64
ML_HEADS = 4
ML_CHUNK = 128
XA_HEADS = 4

V7X_SUBLANES = 8
V7X_LANES = 128
V7X_MXU_DIM = 256
V7X_VMEM_BYTES = 64 * 1024 * 1024
V7X_VMEM_REQUEST_CAP = 56 * 1024 * 1024

SEG_GR, SEG_XR, SEG_Q, SEG_K, SEG_V, SEG_O, SEG_GA, SEG_GB = range(8)
SLOT_YA, SLOT_Q, SLOT_K, SLOT_V, SLOT_O, SLOT_GA, SLOT_GB = range(7)
N_SEG = 8
N_SLOT = 7
IFG_ROWS = 16


def _vmem_limit(n_bytes):
    return int(min(V7X_VMEM_REQUEST_CAP, n_bytes + n_bytes // 2 + (4 << 20)))


def _params(n_grid, vmem_bytes):
    return pltpu.CompilerParams(
        dimension_semantics=("arbitrary",) * n_grid,
        vmem_limit_bytes=_vmem_limit(vmem_bytes),
    )


def _rms_norm(xf, g):
    ms = jnp.mean(xf * xf, axis=-1, keepdims=True)
    return (xf * lax.rsqrt(ms + EPS)) * g


def _softplus(z):
    return jnp.maximum(z, 0.0) + jnp.log1p(jnp.exp(-jnp.abs(z)))


def _row_chunks(n_rows, chunk, body):
    assert n_rows % chunk == 0

    def step(r, carry):
        body(pl.multiple_of(r * chunk, chunk))
        return carry

    lax.fori_loop(0, n_rows // chunk, step, 0)


def _memkv_kernel(mem_ref, g_ref, w_ref, o_ref):
    memn = _rms_norm(mem_ref[0], g_ref[...]).astype(BF16)
    o_ref[0, 0] = jnp.dot(memn, w_ref[0], preferred_element_type=F32).astype(BF16)


def _memkv(mem, mem_g, wkv):
    depth, d, n = wkv.shape
    b, m, _ = mem.shape
    vmem = 2 * (m * d * 4 + d * n * 2 + m * n * 2) + m * n * 4
    return pl.pallas_call(
        _memkv_kernel,
        grid=(depth, b),
        in_specs=[
            pl.BlockSpec((1, m, d), lambda l, i: (i, 0, 0)),
            pl.BlockSpec((1, d), lambda l, i: (0, 0)),
            pl.BlockSpec((1, d, n), lambda l, i: (l, 0, 0)),
        ],
        out_specs=pl.BlockSpec((1, 1, m, n), lambda l, i: (l, i, 0, 0)),
        out_shape=jax.ShapeDtypeStruct((depth, b, m, n), BF16),
        compiler_params=_params(2, vmem),
        name="memkv",
    )(mem, mem_g, wkv)


def _inproj_kernel(x_ref, g_ref, w_ref, wif_ref, bif_ref, cw_ref, cb_ref, wg_ref, bg_ref,
                   lam_ref, seg_ref, ifg_ref,
                   xn_scr, pc_scr, gelu_scr, xc_scr, xcb_scr, gt_scr, a_scr, u_scr,
                   carry_scr, hlast_scr, *, tm, rc):
    i = pl.program_id(1)
    j = pl.program_id(2)
    c = xn_scr.shape[1]
    halo = V7X_SUBLANES

    @pl.when(j == 0)
    def _():
        xn = _rms_norm(x_ref[0], g_ref[0]).astype(BF16)
        xn_scr[...] = xn
        ifg = lax.dot_general(wif_ref[0], xn, (((1,), (1,)), ((), ())),
                              preferred_element_type=F32)
        ifg_ref[0] = ifg + bif_ref[0]

    @pl.when((i == 0) & (j == 0))
    def _():
        carry_scr[...] = jnp.zeros_like(carry_scr)
        hlast_scr[...] = jnp.zeros_like(hlast_scr)

    pc_scr[halo:halo + tm, :] = jnp.dot(xn_scr[...], w_ref[0], preferred_element_type=F32)

    def conv_rows(ci, r0, n):
        blk = pc_scr[pl.ds(r0, n + halo), :]
        cw = cw_ref[0, ci]
        out = cb_ref[0, ci] + blk[halo:halo + n] * cw[3:4]
        for k in range(CONV_W - 1):
            off = halo - (CONV_W - 1) + k
            out = out + blk[off:off + n] * cw[k:k + 1]
        return out

    def load_carry(ci):
        pc_scr[0:halo, :] = carry_scr[ci]

    def save_carry(ci):
        carry_scr[ci] = pc_scr[tm:tm + halo, :]

    @pl.when(j == SEG_GR)
    def _():
        def body(r0):
            gelu_scr[pl.ds(r0, rc), :] = jax.nn.gelu(pc_scr[pl.ds(r0 + halo, rc), :])
        _row_chunks(tm, rc, body)

    @pl.when(j == SEG_XR)
    def _():
        load_carry(0)

        def conv_body(r0):
            xc = conv_rows(0, r0, rc)
            xc_scr[pl.ds(r0, rc), :] = xc
            xcb_scr[pl.ds(r0, rc), :] = xc.astype(BF16)
        _row_chunks(tm, rc, conv_body)
        save_carry(0)

        c_row = -LRU_C * _softplus(-lam_ref[0])
        gw = V7X_MXU_DIM
        for g in range(c // gw):
            sl = slice(g * gw, (g + 1) * gw)
            gt_scr[...] = jnp.dot(xcb_scr[:, sl], wg_ref[0, g], preferred_element_type=F32)
            c_g = c_row[:, sl]
            ba_g = bg_ref[0, 0][:, sl]
            bx_g = bg_ref[0, 1][:, sl]

            def gate_body(r0, sl=sl, c_g=c_g, ba_g=ba_g, bx_g=bx_g):
                gt = gt_scr[pl.ds(r0, rc), :]
                r = jax.nn.sigmoid(gt[:, :gw] + ba_g)
                ig = jax.nn.sigmoid(gt[:, gw:] + bx_g)
                log_a = c_g * r
                a = jnp.exp(log_a)
                mult = jnp.sqrt(-jnp.tanh(log_a) * (a * a + 1.0))
                a_scr[pl.ds(r0, rc), sl] = a
                u_scr[pl.ds(r0, rc), sl] = mult * (ig * xc_scr[pl.ds(r0, rc), sl])
            _row_chunks(tm, rc, gate_body)

        row = lax.broadcasted_iota(jnp.int32, (halo, c), 0)

        def scan_body(r, hprev):
            r0 = pl.multiple_of(r * halo, halo)
            a = a_scr[pl.ds(r0, halo), :]
            u = u_scr[pl.ds(r0, halo), :]
            for k in (1, 2, 4):
                keep = row >= k
                u = jnp.where(keep, u + a * pltpu.roll(u, k, 0), u)
                a = jnp.where(keep, a * pltpu.roll(a, k, 0), a)
            h = u + a * hprev
            u_scr[pl.ds(r0, halo), :] = h
            return jnp.broadcast_to(h[halo - 1:halo, :], (halo, c))

        hlast_scr[...] = lax.fori_loop(0, tm // halo, scan_body, hlast_scr[...])

        def out_body(r0):
            ya = gelu_scr[pl.ds(r0, rc), :] * u_scr[pl.ds(r0, rc), :]
            seg_ref[0, 0, pl.ds(r0, rc), :] = ya.astype(BF16)
        _row_chunks(tm, rc, out_body)

    def conv_silu_segment(ci):
        load_carry(ci)

        def body(r0):
            seg_ref[0, 0, pl.ds(r0, rc), :] = jax.nn.silu(conv_rows(ci, r0, rc)).astype(BF16)
        _row_chunks(tm, rc, body)
        save_carry(ci)

    @pl.when(j == SEG_Q)
    def _():
        conv_silu_segment(1)

    @pl.when(j == SEG_K)
    def _():
        conv_silu_segment(2)

    @pl.when(j >= SEG_V)
    def _():
        def body(r0):
            seg_ref[0, 0, pl.ds(r0, rc), :] = pc_scr[pl.ds(r0 + halo, rc), :].astype(BF16)
        _row_chunks(tm, rc, body)


def _inproj(layer, h, norm_g, w_main, w_ifT, b_if, conv_w, conv_b, w_gate, b_gate, lam, *, tm=512,
            rc=32):
    b, s, d = h.shape
    n_groups = d // V7X_MXU_DIM
    halo = V7X_SUBLANES
    kern = functools.partial(_inproj_kernel, tm=tm, rc=rc)
    vmem = (2 * tm * d * 4 + 2 * d * d * 2 + 2 * tm * d * 2 + 2 * IFG_ROWS * tm * 4
            + tm * d * 2 + (tm + halo) * d * 4 + 4 * tm * d * 4 + tm * d * 2
            + tm * 2 * V7X_MXU_DIM * 4 + 2 * n_groups * V7X_MXU_DIM * 2 * V7X_MXU_DIM * 2
            + 2 * tm * d * 4)
    l = layer
    return pl.pallas_call(
        kern,
        grid=(b, s // tm, N_SEG),
        in_specs=[
            pl.BlockSpec((1, tm, d), lambda bi, i, j: (bi, i, 0)),
            pl.BlockSpec((1, 1, d), lambda bi, i, j: (l, 0, 0)),
            pl.BlockSpec((1, d, d), lambda bi, i, j: (l, 0, j)),
            pl.BlockSpec((1, IFG_ROWS, d), lambda bi, i, j: (l, 0, 0)),
            pl.BlockSpec((1, IFG_ROWS, 1), lambda bi, i, j: (l, 0, 0)),
            pl.BlockSpec((1, 3, CONV_W, d), lambda bi, i, j: (l, 0, 0, 0)),
            pl.BlockSpec((1, 3, 1, d), lambda bi, i, j: (l, 0, 0, 0)),
            pl.BlockSpec((1, n_groups, V7X_MXU_DIM, 2 * V7X_MXU_DIM), lambda bi, i, j: (l, 0, 0, 0)),
            pl.BlockSpec((1, 2, 1, d), lambda bi, i, j: (l, 0, 0, 0)),
            pl.BlockSpec((1, 1, d), lambda bi, i, j: (l, 0, 0)),
        ],
        out_specs=[
            pl.BlockSpec((1, 1, tm, d), lambda bi, i, j: (jnp.maximum(j - 1, 0), bi, i, 0)),
            pl.BlockSpec((1, IFG_ROWS, tm), lambda bi, i, j: (bi, 0, i)),
        ],
        out_shape=[
            jax.ShapeDtypeStruct((N_SLOT, b, s, d), BF16),
            jax.ShapeDtypeStruct((b, IFG_ROWS, s), F32),
        ],
        scratch_shapes=[
            pltpu.VMEM((tm, d), BF16),
            pltpu.VMEM((tm + halo, d), F32),
            pltpu.VMEM((tm, d), F32),
            pltpu.VMEM((tm, d), F32),
            pltpu.VMEM((tm, d), BF16),
            pltpu.VMEM((tm, 2 * V7X_MXU_DIM), F32),
            pltpu.VMEM((tm, d), F32),
            pltpu.VMEM((tm, d), F32),
            pltpu.VMEM((3, halo, d), F32),
            pltpu.VMEM((halo, d), F32),
        ],
        compiler_params=_params(3, vmem),
        name=f"inproj{layer}",
    )(h, norm_g, w_main, w_ifT, b_if, conv_w, conv_b, w_gate, b_gate, lam)


def _lane_scan(x, lane, op, fill):
    n = x.shape[-1]
    k = 1
    while k < n:
        x = op(x, jnp.where(lane >= k, pltpu.roll(x, k, 1), fill))
        k *= 2
    return x


def _mlstm_kernel(q_ref, k_ref, v_ref, o_ref, ifg_ref, ng_ref, y_ref, c_scr, n_scr, m_scr):
    ci = pl.program_id(1)
    L = ML_CHUNK
    dh = c_scr.shape[1]
    rows = V7X_SUBLANES

    @pl.when(ci == 0)
    def _():
        c_scr[...] = jnp.zeros_like(c_scr)
        n_scr[...] = jnp.zeros_like(n_scr)
        m_scr[...] = jnp.zeros_like(m_scr)

    ifg = ifg_ref[0]
    li = ifg[0:rows]
    lf = -_softplus(-ifg[rows:2 * rows])
    lane = lax.broadcasted_iota(jnp.int32, (rows, L), 1)
    bcum = _lane_scan(lf, lane, jnp.add, 0.0)
    m_st = m_scr[...]
    g_tot = jnp.broadcast_to(bcum[:, L - 1:L], (rows, L))
    a_inter = bcum + m_st
    r1 = li - bcum
    m_t = jnp.maximum(a_inter, bcum + _lane_scan(r1, lane, jnp.maximum, -jnp.inf))
    c1 = bcum - m_t
    w_inter = jnp.exp(a_inter - m_t)
    e_negm = jnp.exp(-m_t)
    u = g_tot - bcum + li
    m_next = jnp.maximum(g_tot + m_st, jnp.max(u, axis=-1, keepdims=True))
    decay = jnp.exp(g_tot + m_st - m_next)
    uexp = jnp.exp(u - m_next)
    m_scr[...] = m_next

    stacked = jnp.concatenate(
        [c1, w_inter, e_negm, uexp, jnp.zeros((L - 4 * rows, L), F32)], axis=0)
    cols = stacked.T

    tt = lax.broadcasted_iota(jnp.int32, (L, L), 0)
    ss = lax.broadcasted_iota(jnp.int32, (L, L), 1)
    causal = ss <= tt
    k_scale = dh ** -0.5

    for hd in range(ML_HEADS):
        sl = slice(hd * dh, (hd + 1) * dh)
        qh = q_ref[0, 0, :, sl]
        kh = k_ref[0, 0, :, sl] * jnp.asarray(k_scale, BF16)
        vh = v_ref[0, 0, :, sl]
        c1c = cols[:, hd:hd + 1]
        wic = cols[:, rows + hd:rows + hd + 1]
        enc = cols[:, 2 * rows + hd:2 * rows + hd + 1]
        uec = cols[:, 3 * rows + hd:3 * rows + hd + 1]
        r1r = r1[hd:hd + 1, :]
        dec = decay[hd:hd + 1, 0:1]

        p = jnp.exp(jnp.where(causal, c1c + r1r, -jnp.inf))
        sc = lax.dot_general(qh, kh, (((1,), (1,)), ((), ())),
                             preferred_element_type=F32) * p
        c_old = c_scr[hd]
        n_old = n_scr[hd:hd + 1, :]
        num = wic * jnp.dot(qh, c_old.astype(BF16), preferred_element_type=F32) \
            + jnp.dot(sc.astype(BF16), vh, preferred_element_type=F32)
        qn = jnp.sum(qh.astype(F32) * n_old, axis=-1, keepdims=True)
        den = wic * qn + jnp.sum(sc, axis=-1, keepdims=True)
        hh = num * (1.0 / jnp.maximum(jnp.abs(den), enc))

        wk = kh.astype(F32) * uec
        c_scr[hd] = dec * c_old + lax.dot_general(
            wk.astype(BF16), vh, (((0,), (0,)), ((), ())), preferred_element_type=F32)
        n_scr[hd:hd + 1, :] = dec * n_old + jnp.sum(wk, axis=0, keepdims=True)

        mu = jnp.mean(hh, axis=-1, keepdims=True)
        cen = hh - mu
        var = jnp.mean(cen * cen, axis=-1, keepdims=True)
        hn = (cen * lax.rsqrt(var + EPS)) * ng_ref[0][:, sl]
        gate = jax.nn.sigmoid(o_ref[0, 0, :, sl].astype(F32))
        y_ref[0, :, sl] = (gate * hn).astype(BF16)


def _mlstm(layer, segs, ifg, ml_norm_g):
    _, b, s, d = segs.shape
    L = ML_CHUNK
    dh = d // ML_HEADS
    l = layer

    def seg_spec(slot):
        return pl.BlockSpec((1, 1, L, d), lambda bi, ci: (slot, bi, ci, 0))

    vmem = 2 * (4 * L * d * 2 + IFG_ROWS * L * 4 + L * d * 2) + ML_HEADS * dh * dh * 4 \
        + 16 * L * d * 4
    return pl.pallas_call(
        _mlstm_kernel,
        grid=(b, s // L),
        in_specs=[
            seg_spec(SLOT_Q), seg_spec(SLOT_K), seg_spec(SLOT_V), seg_spec(SLOT_O),
            pl.BlockSpec((1, IFG_ROWS, L), lambda bi, ci: (bi, 0, ci)),
            pl.BlockSpec((1, 1, d), lambda bi, ci: (l, 0, 0)),
        ],
        out_specs=pl.BlockSpec((1, L, d), lambda bi, ci: (bi, ci, 0)),
        out_shape=jax.ShapeDtypeStruct((b, s, d), BF16),
        scratch_shapes=[
            pltpu.VMEM((ML_HEADS, dh, dh), F32),
            pltpu.VMEM((V7X_SUBLANES, dh), F32),
            pltpu.VMEM((V7X_SUBLANES, L), F32),
        ],
        compiler_params=_params(2, vmem),
        name=f"mlstm{layer}",
    )(segs, segs, segs, segs, ifg, ml_norm_g)


def _merge_kernel(ya_ref, yb_ref, ga_ref, gb_ref, h_ref, wa_ref, wb_ref, wm_ref, o_ref):
    pa = jnp.dot(ya_ref[0, 0], wa_ref[0], preferred_element_type=F32)
    pb = jnp.dot(yb_ref[0], wb_ref[0], preferred_element_type=F32)
    y = jax.nn.sigmoid(ga_ref[0, 0].astype(F32)) * pa + jax.nn.sigmoid(gb_ref[0, 0].astype(F32)) * pb
    o_ref[0] = h_ref[0] + jnp.dot(y.astype(BF16), wm_ref[0], preferred_element_type=F32)


def _merge(layer, segs, yb, h, wa, wb, wm, *, tm=512):
    b, s, d = h.shape
    l = layer

    def seg_spec(slot):
        return pl.BlockSpec((1, 1, tm, d), lambda bi, i: (slot, bi, i, 0))

    w_spec = pl.BlockSpec((1, d, d), lambda bi, i: (l, 0, 0))
    row_spec = pl.BlockSpec((1, tm, d), lambda bi, i: (bi, i, 0))
    vmem = 2 * (4 * tm * d * 2 + 2 * tm * d * 4 + 3 * d * d * 2) + 4 * tm * d * 4
    return pl.pallas_call(
        _merge_kernel,
        grid=(b, s // tm),
        in_specs=[seg_spec(SLOT_YA), row_spec, seg_spec(SLOT_GA), seg_spec(SLOT_GB), row_spec,
                  w_spec, w_spec, w_spec],
        out_specs=row_spec,
        out_shape=jax.ShapeDtypeStruct((b, s, d), F32),
        compiler_params=_params(2, vmem),
        name=f"merge{layer}",
    )(segs, yb, segs, segs, h, wa, wb, wm)


def _xattn_kernel(h_ref, g_ref, wq_ref, kv_ref, wo_ref, o_ref, att_scr):
    d = h_ref.shape[2]
    dh = d // XA_HEADS
    hres = h_ref[0]
    xn = _rms_norm(hres, g_ref[0]).astype(BF16)
    q = jnp.dot(xn, wq_ref[0], preferred_element_type=F32).astype(BF16)
    scale = dh ** -0.5
    for hd in range(XA_HEADS):
        sl = slice(hd * dh, (hd + 1) * dh)
        kh = kv_ref[0, 0, :, sl]
        vh = kv_ref[0, 0, :, d + hd * dh:d + (hd + 1) * dh]
        sc = lax.dot_general(q[:, sl], kh, (((1,), (1,)), ((), ())),
                             preferred_element_type=F32) * scale
        e = jnp.exp(sc - jnp.max(sc, axis=-1, keepdims=True))
        p = e * (1.0 / jnp.sum(e, axis=-1, keepdims=True))
        att_scr[:, sl] = jnp.dot(p.astype(BF16), vh, preferred_element_type=F32).astype(BF16)
    o_ref[0] = hres + jnp.dot(att_scr[...], wo_ref[0], preferred_element_type=F32)


def _xattn(layer, h, norm_g, wq, kv, wo, *, tm=512):
    b, s, d = h.shape
    m = kv.shape[2]
    l = layer
    row_spec = pl.BlockSpec((1, tm, d), lambda bi, i: (bi, i, 0))
    w_spec = pl.BlockSpec((1, d, d), lambda bi, i: (l, 0, 0))
    vmem = 2 * (2 * tm * d * 4 + 2 * d * d * 2 + m * 2 * d * 2) + tm * d * 2 + 6 * tm * d * 4
    return pl.pallas_call(
        _xattn_kernel,
        grid=(b, s // tm),
        in_specs=[
            row_spec,
            pl.BlockSpec((1, 1, d), lambda bi, i: (l, 0, 0)),
            w_spec,
            pl.BlockSpec((1, 1, m, 2 * d), lambda bi, i: (l, bi, 0, 0)),
            w_spec,
        ],
        out_specs=row_spec,
        out_shape=jax.ShapeDtypeStruct((b, s, d), F32),
        scratch_shapes=[pltpu.VMEM((tm, d), BF16)],
        compiler_params=_params(2, vmem),
        name=f"xattn{layer}",
    )(h, norm_g, wq, kv, wo)


def _ffn_kernel(x_ref, g_ref, wg_ref, wu_ref, cwg_ref, cwu_ref, cbg_ref, cbu_ref, wd_ref, fg_ref,
                o_ref, xn_scr, acc_scr, pc_scr, act_scr, carry_scr, *, tm, rc, n_j, n_blk,
                final_norm):
    i = pl.program_id(1)
    j = pl.program_id(2)
    halo = V7X_SUBLANES

    @pl.when(j == 0)
    def _():
        xn_scr[...] = _rms_norm(x_ref[0], g_ref[0]).astype(BF16)
        acc_scr[...] = jnp.zeros_like(acc_scr)

    @pl.when(i == 0)
    def _():
        carry_scr[j] = jnp.zeros(carry_scr.shape[1:], F32)

    for half in range(2):
        pc_scr[half, 0:halo, :] = carry_scr[j, half]

    def conv_rows(half, cw_ref, cb_ref, r0):
        blk = pc_scr[half, r0:r0 + rc + halo, :]
        cw = cw_ref[0]
        out = cb_ref[0] + blk[halo:] * cw[FFN_CONV_W - 1:FFN_CONV_W]
        for k in range(FFN_CONV_W - 1):
            shifted = pltpu.roll(blk, FFN_CONV_W - 1 - k, 0)[halo:]
            out = out + shifted * cw[k:k + 1]
        return out

    bm = tm // n_blk

    def up(b):
        for half, w_ref in enumerate((wg_ref, wu_ref)):
            pc_scr[half, halo + b * bm:halo + (b + 1) * bm, :] = jnp.dot(
                xn_scr[b * bm:(b + 1) * bm, :], w_ref[0], preferred_element_type=F32)

    def epilogue(b):
        for r0 in range(b * bm, (b + 1) * bm, rc):
            gpre = conv_rows(0, cwg_ref, cbg_ref, r0)
            upre = conv_rows(1, cwu_ref, cbu_ref, r0)
            act_scr[r0:r0 + rc, :] = (jax.nn.gelu(gpre) * upre).astype(BF16)

    def down(b):
        rows = slice(b * bm, (b + 1) * bm)
        acc_scr[rows, :] += jnp.dot(act_scr[rows, :], wd_ref[0], preferred_element_type=F32)

    up(0)
    for b in range(n_blk):
        if b + 1 < n_blk:
            up(b + 1)
        epilogue(b)
        down(b)

    for half in range(2):
        carry_scr[j, half] = pc_scr[half, tm:tm + halo, :]

    @pl.when(j == n_j - 1)
    def _():
        def out_body(r0):
            res = x_ref[0, pl.ds(r0, rc), :] + acc_scr[pl.ds(r0, rc), :]
            if final_norm:
                res = _rms_norm(res, fg_ref[...])
            o_ref[0, pl.ds(r0, rc), :] = res
        _row_chunks(tm, rc, out_body)


def _ffn(layer, h, norm_g, w_up, conv_w, conv_b, w_down, final_g, *, final_norm, tm=1024, tf=512,
         rc=32, n_blk=4):
    b, s, d = h.shape
    d_ff = w_down.shape[1]
    n_j = d_ff // tf
    halo = V7X_SUBLANES
    l = layer
    kern = functools.partial(_ffn_kernel, tm=tm, rc=rc, n_j=n_j, n_blk=n_blk,
                             final_norm=final_norm)
    row_spec = pl.BlockSpec((1, tm, d), lambda bi, i, j: (bi, i, 0))
    vmem = (4 * tm * d * 4 + 2 * (2 * d * tf * 2 + tf * d * 2) + tm * d * 2 + tm * d * 4
            + 2 * (tm + halo) * tf * 4 + tm * tf * 2 + n_j * 2 * halo * tf * 4 + 2 * tm * tf * 4)
    return pl.pallas_call(
        kern,
        grid=(b, s // tm, n_j),
        in_specs=[
            row_spec,
            pl.BlockSpec((1, 1, d), lambda bi, i, j: (l, 0, 0)),
            pl.BlockSpec((1, d, tf), lambda bi, i, j: (l, 0, j)),
            pl.BlockSpec((1, d, tf), lambda bi, i, j: (l, 0, n_j + j)),
            pl.BlockSpec((1, FFN_CONV_W, tf), lambda bi, i, j: (l, 0, j)),
            pl.BlockSpec((1, FFN_CONV_W, tf), lambda bi, i, j: (l, 0, n_j + j)),
            pl.BlockSpec((1, 1, tf), lambda bi, i, j: (l, 0, j)),
            pl.BlockSpec((1, 1, tf), lambda bi, i, j: (l, 0, n_j + j)),
            pl.BlockSpec((1, tf, d), lambda bi, i, j: (l, j, 0)),
            pl.BlockSpec((1, d), lambda bi, i, j: (0, 0)),
        ],
        out_specs=row_spec,
        out_shape=jax.ShapeDtypeStruct((b, s, d), F32),
        scratch_shapes=[
            pltpu.VMEM((tm, d), BF16),
            pltpu.VMEM((tm, d), F32),
            pltpu.VMEM((2, tm + halo, tf), F32),
            pltpu.VMEM((tm, tf), BF16),
            pltpu.VMEM((n_j, 2, halo, tf), F32),
        ],
        compiler_params=_params(3, vmem),
        name=f"ffn{layer}",
    )(h, norm_g, w_up, w_up, conv_w, conv_w, conv_b, conv_b, w_down, final_g)


def _block_diag_groups(w):
    depth, n_blocks, bs, _ = w.shape
    per = V7X_MXU_DIM // bs
    wg = w.reshape(depth, n_blocks // per, per, bs, bs)
    eye = jnp.eye(per, dtype=w.dtype)
    out = wg[:, :, :, :, None, :] * eye[None, None, :, None, :, None]
    return out.reshape(depth, n_blocks // per, per * bs, per * bs)


def kernel(x, mem, norm_mix_g, w_in, rnn_conv_w, rnn_conv_b, lru_wa, lru_ba, lru_wx, lru_bx, lru_lambda, ml_conv_w, ml_conv_b, ml_if_b, ml_norm_g, w_branch_a, w_branch_b, w_mix_out, norm_xa_g, xa_wq, xa_wkv, xa_wo, norm_ffn_g, ffn_w_up, ffn_conv_w, ffn_conv_b, ffn_w_down, mem_norm_g, final_norm_g):
    depth, d, _ = w_in.shape
    d_ml = ml_norm_g.shape[1]
    assert d == d_ml == lru_lambda.shape[1], "kernels assume D_RNN == D_ML == D_MODEL"
    assert ML_HEADS <= V7X_SUBLANES and x.shape[1] % ML_CHUNK == 0

    o_xr, o_gr, o_q, o_k, o_v, o_o = (n * d for n in range(6))
    o_if = 6 * d
    o_ga = o_if + 2 * ML_HEADS
    o_gb = o_ga + d

    def cols(o):
        return w_in[:, :, o:o + d]

    w_main = jnp.concatenate(
        [cols(o_gr), cols(o_xr), cols(o_q), cols(o_k), cols(o_v), cols(o_o), cols(o_ga), cols(o_gb)],
        axis=-1).astype(BF16)
    w_if = jnp.swapaxes(w_in[:, :, o_if:o_ga], 1, 2)
    pad = jnp.zeros((depth, V7X_SUBLANES - ML_HEADS, d), w_in.dtype)
    w_ifT = jnp.concatenate([w_if[:, :ML_HEADS], pad, w_if[:, ML_HEADS:], pad], axis=1).astype(BF16)
    bpad = jnp.zeros((depth, V7X_SUBLANES - ML_HEADS), ml_if_b.dtype)
    b_if = jnp.concatenate([ml_if_b[:, :ML_HEADS], bpad, ml_if_b[:, ML_HEADS:], bpad], axis=1)[..., None]

    conv_w = jnp.stack([rnn_conv_w, ml_conv_w[:, :, :d], ml_conv_w[:, :, d:]], axis=1)
    conv_b = jnp.stack([rnn_conv_b, ml_conv_b[:, :d], ml_conv_b[:, d:]], axis=1)[:, :, None, :]
    w_gate = jnp.concatenate([_block_diag_groups(lru_wa), _block_diag_groups(lru_wx)], axis=-1).astype(BF16)
    b_gate = jnp.stack([lru_ba, lru_bx], axis=1)[:, :, None, :]

    def row(g):
        return g[:, None, :]

    wa, wb, wm = (w.astype(BF16) for w in (w_branch_a, w_branch_b, w_mix_out))
    wq, wkv, wo = (w.astype(BF16) for w in (xa_wq, xa_wkv, xa_wo))
    w_up, w_down = ffn_w_up.astype(BF16), ffn_w_down.astype(BF16)
    ffn_cb = row(ffn_conv_b)

    kv = _memkv(mem, mem_norm_g[None, :], wkv)
    h = x
    for l in range(depth):
        segs, ifg = _inproj(l, h, row(norm_mix_g), w_main, w_ifT, b_if, conv_w, conv_b, w_gate,
                            b_gate, row(lru_lambda))
        yb = _mlstm(l, segs, ifg, row(ml_norm_g))
        h = _merge(l, segs, yb, h, wa, wb, wm)
        h = _xattn(l, h, row(norm_xa_g), wq, kv, wo)
        h = _ffn(l, h, row(norm_ffn_g), w_up, ffn_conv_w, ffn_cb, w_down, final_norm_g[None, :],
                 final_norm=(l == depth - 1))
    return h
```

```python
import functools

import jax
import jax.numpy as jnp
from jax import lax
from jax.experimental import pallas as pl
from jax.experimental.pallas import tpu as pltpu

F32 = jnp.float32
BF16 = jnp.bfloat16

EPS = 1e-6
LRU_C = 8.0
CONV_W = 4
FFN_CONV_W = 3
RNN_BLOCK = 64
ML_HEADS = 4
ML_CHUNK = 128
XA_HEADS = 4

V7X_SUBLANES = 8
V7X_LANES = 128
V7X_MXU_DIM = 256
V7X_VMEM_BYTES = 64 * 1024 * 1024
V7X_VMEM_REQUEST_CAP = 56 * 1024 * 1024

SEG_GR, SEG_XR, SEG_Q, SEG_K, SEG_V, SEG_O, SEG_GA, SEG_GB = range(8)
SLOT_YA, SLOT_Q, SLOT_K, SLOT_V, SLOT_O, SLOT_GA, SLOT_GB = range(7)
N_SEG = 8
N_SLOT = 7
IFG_ROWS = 16


def _vmem_limit(n_bytes):
    return int(min(V7X_VMEM_REQUEST_CAP, n_bytes + n_bytes // 2 + (4 << 20)))


def _params(n_grid, vmem_bytes):
    return pltpu.CompilerParams(
        dimension_semantics=("arbitrary",) * n_grid,
        vmem_limit_bytes=_vmem_limit(vmem_bytes),
    )


def _rms_norm(xf, g):
    ms = jnp.mean(xf * xf, axis=-1, keepdims=True)
    return (xf * lax.rsqrt(ms + EPS)) * g


def _softplus(z):
    return jnp.maximum(z, 0.0) + jnp.log1p(jnp.exp(-jnp.abs(z)))


def _row_chunks(n_rows, chunk, body):
    assert n_rows % chunk == 0

    def step(r, carry):
        body(pl.multiple_of(r * chunk, chunk))
        return carry

    lax.fori_loop(0, n_rows // chunk, step, 0)


TOK_BLOCK = ML_CHUNK
TOK_SEG = TOK_BLOCK // V7X_SUBLANES


def _to_block_order(t):
    b, s, d = t.shape
    t = t.reshape(b, s // TOK_BLOCK, V7X_SUBLANES, TOK_SEG, d)
    return jnp.swapaxes(t, 2, 3).reshape(b, s, d)


def _from_block_order(t):
    b, s, d = t.shape
    t = t.reshape(b, s // TOK_BLOCK, TOK_SEG, V7X_SUBLANES, d)
    return jnp.swapaxes(t, 2, 3).reshape(b, s, d)


def _time_of_row(row):
    return (row % V7X_SUBLANES) * TOK_SEG + row // V7X_SUBLANES


def _row_of_time(tau):
    return (tau % TOK_SEG) * V7X_SUBLANES + tau // TOK_SEG


def _halo_vregs(prev_tail, cur_tail):
    sub = lax.broadcasted_iota(jnp.int32, (V7X_SUBLANES, cur_tail.shape[1]), 0)
    out = []
    for i in range(cur_tail.shape[0] // V7X_SUBLANES):
        rows = slice(i * V7X_SUBLANES, (i + 1) * V7X_SUBLANES)
        mixed = jnp.where(sub == V7X_SUBLANES - 1, prev_tail[rows], cur_tail[rows])
        out.append(pltpu.roll(mixed, 1, 0))
    return out


def _causal_conv_run(prev_tail, cur, cw, cb):
    taps = cw.shape[0]
    n = cur.shape[0]
    tail = (taps - 1) * V7X_SUBLANES
    ext = jnp.concatenate(_halo_vregs(prev_tail, cur[n - tail:]) + [cur], axis=0)
    out = cb
    for k in range(taps):
        out = out + ext[k * V7X_SUBLANES:k * V7X_SUBLANES + n] * cw[k:k + 1]
    return out


def _memkv_kernel(mem_ref, g_ref, w_ref, o_ref):
    memn = _rms_norm(mem_ref[0], g_ref[...]).astype(BF16)
    o_ref[0, 0] = jnp.dot(memn, w_ref[0], preferred_element_type=F32).astype(BF16)


def _memkv(mem, mem_g, wkv):
    depth, d, n = wkv.shape
    b, m, _ = mem.shape
    vmem = 2 * (m * d * 4 + d * n * 2 + m * n * 2) + m * n * 4
    return pl.pallas_call(
        _memkv_kernel,
        grid=(depth, b),
        in_specs=[
            pl.BlockSpec((1, m, d), lambda l, i: (i, 0, 0)),
            pl.BlockSpec((1, d), lambda l, i: (0, 0)),
            pl.BlockSpec((1, d, n), lambda l, i: (l, 0, 0)),
        ],
        out_specs=pl.BlockSpec((1, 1, m, n), lambda l, i: (l, i, 0, 0)),
        out_shape=jax.ShapeDtypeStruct((depth, b, m, n), BF16),
        compiler_params=_params(2, vmem),
        name="memkv",
    )(mem, mem_g, wkv)


def _inproj_kernel(x_ref, g_ref, w_ref, wif_ref, bif_ref, cw_ref, cb_ref, wg_ref, bg_ref,
                   lam_ref, seg_ref, ifg_ref,
                   xn_scr, pc_scr, gelu_scr, xc_scr, xcb_scr, a_scr, u_scr,
                   carry_scr, hlast_scr, *, tm, rc, n_blk, n_steps):
    m = pl.program_id(1)
    seg_done = (m + N_SEG - 1) % N_SEG
    c = xn_scr.shape[1]
    halo = (CONV_W - 1) * V7X_SUBLANES
    gw = V7X_MXU_DIM
    chunks = range(0, tm, rc)
    runs = [(r0, slice(l0, l0 + V7X_LANES))
            for r0 in range(0, tm, TOK_BLOCK) for l0 in range(0, c, V7X_LANES)]

    def norm_and_gates():
        xn = _rms_norm(x_ref[0], g_ref[0]).astype(BF16)
        xn_scr[...] = xn
        ifg = lax.dot_general(wif_ref[0], xn, (((1,), (1,)), ((), ())),
                              preferred_element_type=F32)
        ifg_ref[0] = ifg + bif_ref[0]

    bm = tm // n_blk

    def project_rows(par, b):
        pc_scr[par, halo + b * bm:halo + (b + 1) * bm, :] = jnp.dot(
            xn_scr[b * bm:(b + 1) * bm, :], w_ref[0], preferred_element_type=F32)

    def project(par):
        for b in range(n_blk):
            project_rows(par, b)

    def interleave(par_next, items):
        n = len(items)
        for b in range(n_blk):
            project_rows(par_next, b)
            for item in items[b * n // n_blk:(b + 1) * n // n_blk]:
                item()

    def conv_run(par, ci, r0, lanes):
        base = halo + r0
        return _causal_conv_run(pc_scr[par, base - halo:base, lanes],
                                pc_scr[par, base:base + TOK_BLOCK, lanes],
                                cw_ref[0, ci][:, lanes], cb_ref[0, ci][:, lanes])

    def load_carry(par, ci):
        pc_scr[par, 0:halo, :] = carry_scr[ci]

    def save_carry(par, ci):
        carry_scr[ci] = pc_scr[par, tm:tm + halo, :]

    def gate_branch(par):
        def item(r0):
            gelu_scr[r0:r0 + rc, :] = jax.nn.gelu(pc_scr[par, halo + r0:halo + r0 + rc, :])
        return [functools.partial(item, r0) for r0 in chunks]

    def rg_lru(par):
        items = [functools.partial(load_carry, par, 0)]

        def conv_item(r0, lanes):
            xc = conv_run(par, 0, r0, lanes)
            xc_scr[r0:r0 + TOK_BLOCK, lanes] = xc
            xcb_scr[r0:r0 + TOK_BLOCK, lanes] = xc.astype(BF16)
        items += [functools.partial(conv_item, r0, lanes) for r0, lanes in runs]
        items.append(functools.partial(save_carry, par, 0))

        state = {}

        def gate_dot(g):
            sl = slice(g * gw, (g + 1) * gw)
            state["gt"] = jnp.dot(xcb_scr[:, sl], wg_ref[0, g], preferred_element_type=F32)
            state["c_row"] = -LRU_C * _softplus(-lam_ref[0][:, sl])

        def gate_item(g, r0):
            sl = slice(g * gw, (g + 1) * gw)
            gt = state["gt"]
            r = jax.nn.sigmoid(gt[r0:r0 + rc, :gw] + bg_ref[0, 0][:, sl])
            ig = jax.nn.sigmoid(gt[r0:r0 + rc, gw:] + bg_ref[0, 1][:, sl])
            log_a = state["c_row"] * r
            a = jnp.exp(log_a)
            mult = jnp.sqrt(-jnp.tanh(log_a) * (a * a + 1.0))
            a_scr[r0:r0 + rc, sl] = a
            u_scr[r0:r0 + rc, sl] = mult * (ig * xc_scr[r0:r0 + rc, sl])

        for g in range(c // gw):
            items.append(functools.partial(gate_dot, g))
            items += [functools.partial(gate_item, g, r0) for r0 in chunks]

        def scan_item(r0, lanes):
            nsub = V7X_SUBLANES
            sub = lax.broadcasted_iota(jnp.int32, (nsub, V7X_LANES), 0)
            key = ("h", lanes.start)
            hprev = hlast_scr[:, lanes] if r0 == 0 else state[key]
            local, decay = [], []
            for r in range(TOK_SEG):
                rows = slice(r0 + r * nsub, r0 + (r + 1) * nsub)
                a = a_scr[rows, lanes]
                u = u_scr[rows, lanes]
                local.append(u if r == 0 else a * local[-1] + u)
                decay.append(a if r == 0 else a * decay[-1])
            ea, eu = decay[-1], local[-1]
            for k in (1, 2, 4):
                keep = sub >= k
                eu = jnp.where(keep, eu + ea * pltpu.roll(eu, k, 0), eu)
                ea = jnp.where(keep, ea * pltpu.roll(ea, k, 0), ea)
            ends = eu + ea * hprev
            h_in = pltpu.roll(jnp.where(sub == nsub - 1, hprev, ends), 1, 0)
            for r in range(TOK_SEG):
                rows = slice(r0 + r * nsub, r0 + (r + 1) * nsub)
                u_scr[rows, lanes] = local[r] + decay[r] * h_in
            hnew = jnp.broadcast_to(ends[nsub - 1:nsub, :], (nsub, V7X_LANES))
            state[key] = hnew
            if r0 + TOK_BLOCK == tm:
                hlast_scr[:, lanes] = hnew

        def out_item(r0):
            ya = gelu_scr[r0:r0 + rc, :] * u_scr[r0:r0 + rc, :]
            seg_ref[0, 0, r0:r0 + rc, :] = ya.astype(BF16)

        items += [functools.partial(scan_item, r0, lanes) for r0, lanes in runs]
        items += [functools.partial(out_item, r0) for r0 in chunks]
        return items

    def conv_silu(ci, par):
        def item(r0, lanes):
            seg_ref[0, 0, r0:r0 + TOK_BLOCK, lanes] = jax.nn.silu(
                conv_run(par, ci, r0, lanes)).astype(BF16)
        return ([functools.partial(load_carry, par, ci)]
                + [functools.partial(item, r0, lanes) for r0, lanes in runs]
                + [functools.partial(save_carry, par, ci)])

    def plain(par):
        def item(r0):
            seg_ref[0, 0, r0:r0 + rc, :] = pc_scr[par, halo + r0:halo + r0 + rc, :].astype(BF16)
        return [functools.partial(item, r0) for r0 in chunks]

    epilogues = {
        SEG_GR: gate_branch,
        SEG_XR: rg_lru,
        SEG_Q: functools.partial(conv_silu, 1),
        SEG_K: functools.partial(conv_silu, 2),
    }

    @pl.when(m == 0)
    def _():
        carry_scr[...] = jnp.zeros_like(carry_scr)
        hlast_scr[...] = jnp.zeros_like(hlast_scr)
        norm_and_gates()
        project(0)

    for j in range(N_SEG):
        @pl.when((m > 0) & (seg_done == j))
        def _(j=j):
            par = j % 2
            items = epilogues.get(j, plain)(par)
            if j + 1 < N_SEG:
                interleave(1 - par, items)
            else:
                for item in items:
                    item()

                @pl.when(m < n_steps)
                def _():
                    norm_and_gates()
                    project(1 - par)


def _inproj(layer, h, norm_g, w_main, w_ifT, b_if, conv_w, conv_b, w_gate, b_gate, lam, *, tm=512,
            rc=32, n_blk=4):
    b, s, d = h.shape
    n_groups = d // V7X_MXU_DIM
    halo = (CONV_W - 1) * V7X_SUBLANES
    assert tm % TOK_BLOCK == 0
    n_tiles = s // tm
    n_steps = n_tiles * N_SEG
    kern = functools.partial(_inproj_kernel, tm=tm, rc=rc, n_blk=n_blk, n_steps=n_steps)
    vmem = (2 * tm * d * 4 + 2 * d * d * 2 + 2 * tm * d * 2 + 2 * IFG_ROWS * tm * 4
            + tm * d * 2 + 2 * (tm + halo) * d * 4 + 4 * tm * d * 4 + tm * d * 2
            + 2 * n_groups * V7X_MXU_DIM * 2 * V7X_MXU_DIM * 2 + 3 * tm * d * 4)
    l = layer

    def proj_tile(m):
        return jnp.minimum(m // N_SEG, n_tiles - 1)

    def done_tile(m):
        return jnp.maximum(m - 1, 0) // N_SEG

    def done_slot(m):
        return jnp.maximum(jnp.maximum(m - 1, 0) % N_SEG - 1, 0)

    return pl.pallas_call(
        kern,
        grid=(b, n_steps + 1),
        in_specs=[
            pl.BlockSpec((1, tm, d), lambda bi, m: (bi, proj_tile(m), 0)),
            pl.BlockSpec((1, 1, d), lambda bi, m: (l, 0, 0)),
            pl.BlockSpec((1, d, d), lambda bi, m: (l, 0, m % N_SEG)),
            pl.BlockSpec((1, IFG_ROWS, d), lambda bi, m: (l, 0, 0)),
            pl.BlockSpec((1, IFG_ROWS, 1), lambda bi, m: (l, 0, 0)),
            pl.BlockSpec((1, 3, CONV_W, d), lambda bi, m: (l, 0, 0, 0)),
            pl.BlockSpec((1, 3, 1, d), lambda bi, m: (l, 0, 0, 0)),
            pl.BlockSpec((1, n_groups, V7X_MXU_DIM, 2 * V7X_MXU_DIM), lambda bi, m: (l, 0, 0, 0)),
            pl.BlockSpec((1, 2, 1, d), lambda bi, m: (l, 0, 0, 0)),
            pl.BlockSpec((1, 1, d), lambda bi, m: (l, 0, 0)),
        ],
        out_specs=[
            pl.BlockSpec((1, 1, tm, d), lambda bi, m: (done_slot(m), bi, done_tile(m), 0)),
            pl.BlockSpec((1, IFG_ROWS, tm), lambda bi, m: (bi, 0, proj_tile(m))),
        ],
        out_shape=[
            jax.ShapeDtypeStruct((N_SLOT, b, s, d), BF16),
            jax.ShapeDtypeStruct((b, IFG_ROWS, s), F32),
        ],
        scratch_shapes=[
            pltpu.VMEM((tm, d), BF16),
            pltpu.VMEM((2, tm + halo, d), F32),
            pltpu.VMEM((tm, d), F32),
            pltpu.VMEM((tm, d), F32),
            pltpu.VMEM((tm, d), BF16),
            pltpu.VMEM((tm, d), F32),
            pltpu.VMEM((tm, d), F32),
            pltpu.VMEM((3, halo, d), F32),
            pltpu.VMEM((V7X_SUBLANES, d), F32),
        ],
        compiler_params=_params(2, vmem),
        name=f"inproj{layer}",
    )(h, norm_g, w_main, w_ifT, b_if, conv_w, conv_b, w_gate, b_gate, lam)


def _lane_scan(x, lane, op, fill):
    n = x.shape[-1]
    k = 1
    while k < n:
        x = op(x, jnp.where(lane >= k, pltpu.roll(x, k, 1), fill))
        k *= 2
    return x


def _mlstm_kernel(q_ref, k_ref, v_ref, o_ref, ifg_ref, ng_ref, y_ref, c_scr, n_scr, m_scr):
    @pl.when(pl.program_id(1) == 0)
    def _():
        c_scr[...] = jnp.zeros_like(c_scr)
        n_scr[...] = jnp.zeros_like(n_scr)
        m_scr[...] = jnp.zeros_like(m_scr)

    nb = y_ref.shape[0]
    gates = [_mlstm_gates(bb, ifg_ref, m_scr) for bb in range(nb)]
    for hd in range(ML_HEADS):
        for bb in range(nb):
            _mlstm_head(bb, hd, gates[bb], q_ref, k_ref, v_ref, o_ref, ng_ref, y_ref, c_scr, n_scr)


def _mlstm_gates(bb, ifg_ref, m_scr):
    L = ML_CHUNK
    rows = V7X_SUBLANES

    lane = lax.broadcasted_iota(jnp.int32, (rows, L), 1)
    to_time = _row_of_time(lane)
    to_rows = _time_of_row(lane)

    def as_time(x):
        return jnp.take_along_axis(x, to_time, axis=1)

    def as_rows(x):
        return jnp.take_along_axis(x, to_rows, axis=1)

    ifg = ifg_ref[bb]
    li = as_time(ifg[0:rows])
    lf = -_softplus(-as_time(ifg[rows:2 * rows]))
    bcum = _lane_scan(lf, lane, jnp.add, 0.0)
    m_st = m_scr[bb]
    g_tot = jnp.broadcast_to(bcum[:, L - 1:L], (rows, L))
    a_inter = bcum + m_st
    r1 = li - bcum
    m_t = jnp.maximum(a_inter, bcum + _lane_scan(r1, lane, jnp.maximum, -jnp.inf))
    c1 = bcum - m_t
    w_inter = jnp.exp(a_inter - m_t)
    e_negm = jnp.exp(-m_t)
    u = g_tot - bcum + li
    m_next = jnp.maximum(g_tot + m_st, jnp.max(u, axis=-1, keepdims=True))
    decay = jnp.exp(g_tot + m_st - m_next)
    uexp = jnp.exp(u - m_next)
    m_scr[bb] = m_next

    stacked = jnp.concatenate(
        [as_rows(c1), as_rows(w_inter), as_rows(e_negm), as_rows(uexp),
         jnp.zeros((L - 4 * rows, L), F32)], axis=0)
    cols = stacked.T
    return cols, as_rows(r1), decay


def _mlstm_head(bb, hd, gates, q_ref, k_ref, v_ref, o_ref, ng_ref, y_ref, c_scr, n_scr):
    L = ML_CHUNK
    dh = c_scr.shape[2]
    rows = V7X_SUBLANES
    cols, r1, decay = gates
    tt = lax.broadcasted_iota(jnp.int32, (L, L), 0)
    ss = lax.broadcasted_iota(jnp.int32, (L, L), 1)
    causal = _time_of_row(ss) <= _time_of_row(tt)

    sl = slice(hd * dh, (hd + 1) * dh)
    qh = q_ref[0, bb, :, sl]
    kh = k_ref[0, bb, :, sl] * jnp.asarray(dh ** -0.5, BF16)
    vh = v_ref[0, bb, :, sl]
    c1c = cols[:, hd:hd + 1]
    wic = cols[:, rows + hd:rows + hd + 1]
    enc = cols[:, 2 * rows + hd:2 * rows + hd + 1]
    uec = cols[:, 3 * rows + hd:3 * rows + hd + 1]
    r1r = r1[hd:hd + 1, :]
    dec = decay[hd:hd + 1, 0:1]

    p = jnp.exp(jnp.where(causal, c1c + r1r, -jnp.inf))
    sc = lax.dot_general(qh, kh, (((1,), (1,)), ((), ())),
                         preferred_element_type=F32) * p
    c_old = c_scr[bb, hd]
    n_old = n_scr[bb, hd:hd + 1, :]
    num = wic * jnp.dot(qh, c_old.astype(BF16), preferred_element_type=F32) \
        + jnp.dot(sc.astype(BF16), vh, preferred_element_type=F32)
    qn = jnp.sum(qh.astype(F32) * n_old, axis=-1, keepdims=True)
    den = wic * qn + jnp.sum(sc, axis=-1, keepdims=True)
    hh = num * (1.0 / jnp.maximum(jnp.abs(den), enc))

    wk = kh.astype(F32) * uec
    c_scr[bb, hd] = dec * c_old + lax.dot_general(
        wk.astype(BF16), vh, (((0,), (0,)), ((), ())), preferred_element_type=F32)
    n_scr[bb, hd:hd + 1, :] = dec * n_old + jnp.sum(wk, axis=0, keepdims=True)

    mu = jnp.mean(hh, axis=-1, keepdims=True)
    cen = hh - mu
    var = jnp.mean(cen * cen, axis=-1, keepdims=True)
    hn = (cen * lax.rsqrt(var + EPS)) * ng_ref[0][:, sl]
    gate = jax.nn.sigmoid(o_ref[0, bb, :, sl].astype(F32))
    y_ref[bb, :, sl] = (gate * hn).astype(BF16)


def _mlstm(layer, segs, ifg, ml_norm_g, *, nb=2):
    _, b, s, d = segs.shape
    L = ML_CHUNK
    dh = d // ML_HEADS
    l = layer

    def seg_spec(slot):
        return pl.BlockSpec((1, nb, L, d), lambda bi, ci: (slot, bi, ci, 0))

    vmem = nb * (2 * (4 * L * d * 2 + IFG_ROWS * L * 4 + L * d * 2) + ML_HEADS * dh * dh * 4
                 + 16 * L * d * 4)
    return pl.pallas_call(
        _mlstm_kernel,
        grid=(b // nb, s // L),
        in_specs=[
            seg_spec(SLOT_Q), seg_spec(SLOT_K), seg_spec(SLOT_V), seg_spec(SLOT_O),
            pl.BlockSpec((nb, IFG_ROWS, L), lambda bi, ci: (bi, 0, ci)),
            pl.BlockSpec((1, 1, d), lambda bi, ci: (l, 0, 0)),
        ],
        out_specs=pl.BlockSpec((nb, L, d), lambda bi, ci: (bi, ci, 0)),
        out_shape=jax.ShapeDtypeStruct((b, s, d), BF16),
        scratch_shapes=[
            pltpu.VMEM((nb, ML_HEADS, dh, dh), F32),
            pltpu.VMEM((nb, V7X_SUBLANES, dh), F32),
            pltpu.VMEM((nb, V7X_SUBLANES, L), F32),
        ],
        compiler_params=_params(2, vmem),
        name=f"mlstm{layer}",
    )(segs, segs, segs, segs, ifg, ml_norm_g)


def _merge_kernel(ya_ref, yb_ref, ga_ref, gb_ref, h_ref, wa_ref, wb_ref, wm_ref, o_ref):
    pa = jnp.dot(ya_ref[0, 0], wa_ref[0], preferred_element_type=F32)
    pb = jnp.dot(yb_ref[0], wb_ref[0], preferred_element_type=F32)
    y = jax.nn.sigmoid(ga_ref[0, 0].astype(F32)) * pa + jax.nn.sigmoid(gb_ref[0, 0].astype(F32)) * pb
    o_ref[0] = h_ref[0] + jnp.dot(y.astype(BF16), wm_ref[0], preferred_element_type=F32)


def _merge(layer, segs, yb, h, wa, wb, wm, *, tm=512):
    b, s, d = h.shape
    l = layer

    def seg_spec(slot):
        return pl.BlockSpec((1, 1, tm, d), lambda bi, i: (slot, bi, i, 0))

    w_spec = pl.BlockSpec((1, d, d), lambda bi, i: (l, 0, 0))
    row_spec = pl.BlockSpec((1, tm, d), lambda bi, i: (bi, i, 0))
    vmem = 2 * (4 * tm * d * 2 + 2 * tm * d * 4 + 3 * d * d * 2) + 4 * tm * d * 4
    return pl.pallas_call(
        _merge_kernel,
        grid=(b, s // tm),
        in_specs=[seg_spec(SLOT_YA), row_spec, seg_spec(SLOT_GA), seg_spec(SLOT_GB), row_spec,
                  w_spec, w_spec, w_spec],
        out_specs=row_spec,
        out_shape=jax.ShapeDtypeStruct((b, s, d), F32),
        compiler_params=_params(2, vmem),
        name=f"merge{layer}",
    )(segs, yb, segs, segs, h, wa, wb, wm)


def _xattn_kernel(h_ref, g_ref, wq_ref, kv_ref, wo_ref, o_ref, att_scr):
    d = h_ref.shape[2]
    dh = d // XA_HEADS
    hres = h_ref[0]
    xn = _rms_norm(hres, g_ref[0]).astype(BF16)
    q = jnp.dot(xn, wq_ref[0], preferred_element_type=F32).astype(BF16)
    scale = dh ** -0.5
    for hd in range(XA_HEADS):
        sl = slice(hd * dh, (hd + 1) * dh)
        kh = kv_ref[0, 0, :, sl]
        vh = kv_ref[0, 0, :, d + hd * dh:d + (hd + 1) * dh]
        sc = lax.dot_general(q[:, sl], kh, (((1,), (1,)), ((), ())),
                             preferred_element_type=F32) * scale
        e = jnp.exp(sc - jnp.max(sc, axis=-1, keepdims=True))
        p = e * (1.0 / jnp.sum(e, axis=-1, keepdims=True))
        att_scr[:, sl] = jnp.dot(p.astype(BF16), vh, preferred_element_type=F32).astype(BF16)
    o_ref[0] = hres + jnp.dot(att_scr[...], wo_ref[0], preferred_element_type=F32)


def _xattn(layer, h, norm_g, wq, kv, wo, *, tm=512):
    b, s, d = h.shape
    m = kv.shape[2]
    l = layer
    row_spec = pl.BlockSpec((1, tm, d), lambda bi, i: (bi, i, 0))
    w_spec = pl.BlockSpec((1, d, d), lambda bi, i: (l, 0, 0))
    vmem = 2 * (2 * tm * d * 4 + 2 * d * d * 2 + m * 2 * d * 2) + tm * d * 2 + 6 * tm * d * 4
    return pl.pallas_call(
        _xattn_kernel,
        grid=(b, s // tm),
        in_specs=[
            row_spec,
            pl.BlockSpec((1, 1, d), lambda bi, i: (l, 0, 0)),
            w_spec,
            pl.BlockSpec((1, 1, m, 2 * d), lambda bi, i: (l, bi, 0, 0)),
            w_spec,
        ],
        out_specs=row_spec,
        out_shape=jax.ShapeDtypeStruct((b, s, d), F32),
        scratch_shapes=[pltpu.VMEM((tm, d), BF16)],
        compiler_params=_params(2, vmem),
        name=f"xattn{layer}",
    )(h, norm_g, wq, kv, wo)


def _ffn_kernel(x_ref, g_ref, wg_ref, wu_ref, cwg_ref, cwu_ref, cbg_ref, cbu_ref, wd_ref, fg_ref,
                o_ref, xn_scr, acc_scr, pc_scr, act_scr, carry_scr, *, tm, rc, n_j, n_blk,
                final_norm):
    i = pl.program_id(1)
    j = pl.program_id(2)
    halo = (FFN_CONV_W - 1) * V7X_SUBLANES
    tf = act_scr.shape[1]

    @pl.when(j == 0)
    def _():
        xn_scr[...] = _rms_norm(x_ref[0], g_ref[0]).astype(BF16)
        acc_scr[...] = jnp.zeros_like(acc_scr)

    @pl.when(i == 0)
    def _():
        carry_scr[j] = jnp.zeros(carry_scr.shape[1:], F32)

    for half in range(2):
        pc_scr[half, 0:halo, :] = carry_scr[j, half]

    def conv_run(half, cw_ref, cb_ref, r0, lanes):
        base = halo + r0
        return _causal_conv_run(pc_scr[half, base - halo:base, lanes],
                                pc_scr[half, base:base + TOK_BLOCK, lanes],
                                cw_ref[0][:, lanes], cb_ref[0][:, lanes])

    bm = tm // n_blk

    def up(b):
        for half, w_ref in enumerate((wg_ref, wu_ref)):
            pc_scr[half, halo + b * bm:halo + (b + 1) * bm, :] = jnp.dot(
                xn_scr[b * bm:(b + 1) * bm, :], w_ref[0], preferred_element_type=F32)

    def epilogue(b):
        for r0 in range(b * bm, (b + 1) * bm, TOK_BLOCK):
            for l0 in range(0, tf, V7X_LANES):
                lanes = slice(l0, l0 + V7X_LANES)
                gpre = conv_run(0, cwg_ref, cbg_ref, r0, lanes)
                upre = conv_run(1, cwu_ref, cbu_ref, r0, lanes)
                act_scr[r0:r0 + TOK_BLOCK, lanes] = (jax.nn.gelu(gpre) * upre).astype(BF16)

    def down(b):
        rows = slice(b * bm, (b + 1) * bm)
        acc_scr[rows, :] += jnp.dot(act_scr[rows, :], wd_ref[0], preferred_element_type=F32)

    up(0)
    for b in range(n_blk):
        if b + 1 < n_blk:
            up(b + 1)
        epilogue(b)
        down(b)

    for half in range(2):
        carry_scr[j, half] = pc_scr[half, tm:tm + halo, :]

    @pl.when(j == n_j - 1)
    def _():
        def out_body(r0):
            res = x_ref[0, pl.ds(r0, rc), :] + acc_scr[pl.ds(r0, rc), :]
            if final_norm:
                res = _rms_norm(res, fg_ref[...])
            o_ref[0, pl.ds(r0, rc), :] = res
        _row_chunks(tm, rc, out_body)


def _ffn(layer, h, norm_g, w_up, conv_w, conv_b, w_down, final_g, *, final_norm, tm=1024, tf=512,
         rc=32, n_blk=4):
    b, s, d = h.shape
    d_ff = w_down.shape[1]
    n_j = d_ff // tf
    halo = (FFN_CONV_W - 1) * V7X_SUBLANES
    assert (tm // n_blk) % TOK_BLOCK == 0
    l = layer
    kern = functools.partial(_ffn_kernel, tm=tm, rc=rc, n_j=n_j, n_blk=n_blk,
                             final_norm=final_norm)
    row_spec = pl.BlockSpec((1, tm, d), lambda bi, i, j: (bi, i, 0))
    vmem = (4 * tm * d * 4 + 2 * (2 * d * tf * 2 + tf * d * 2) + tm * d * 2 + tm * d * 4
            + 2 * (tm + halo) * tf * 4 + tm * tf * 2 + n_j * 2 * halo * tf * 4 + 2 * tm * tf * 4)
    return pl.pallas_call(
        kern,
        grid=(b, s // tm, n_j),
        in_specs=[
            row_spec,
            pl.BlockSpec((1, 1, d), lambda bi, i, j: (l, 0, 0)),
            pl.BlockSpec((1, d, tf), lambda bi, i, j: (l, 0, j)),
            pl.BlockSpec((1, d, tf), lambda bi, i, j: (l, 0, n_j + j)),
            pl.BlockSpec((1, FFN_CONV_W, tf), lambda bi, i, j: (l, 0, j)),
            pl.BlockSpec((1, FFN_CONV_W, tf), lambda bi, i, j: (l, 0, n_j + j)),
            pl.BlockSpec((1, 1, tf), lambda bi, i, j: (l, 0, j)),
            pl.BlockSpec((1, 1, tf), lambda bi, i, j: (l, 0, n_j + j)),
            pl.BlockSpec((1, tf, d), lambda bi, i, j: (l, j, 0)),
            pl.BlockSpec((1, d), lambda bi, i, j: (0, 0)),
        ],
        out_specs=row_spec,
        out_shape=jax.ShapeDtypeStruct((b, s, d), F32),
        scratch_shapes=[
            pltpu.VMEM((tm, d), BF16),
            pltpu.VMEM((tm, d), F32),
            pltpu.VMEM((2, tm + halo, tf), F32),
            pltpu.VMEM((tm, tf), BF16),
            pltpu.VMEM((n_j, 2, halo, tf), F32),
        ],
        compiler_params=_params(3, vmem),
        name=f"ffn{layer}",
    )(h, norm_g, w_up, w_up, conv_w, conv_w, conv_b, conv_b, w_down, final_g)


def _block_diag_groups(w):
    depth, n_blocks, bs, _ = w.shape
    per = V7X_MXU_DIM // bs
    wg = w.reshape(depth, n_blocks // per, per, bs, bs)
    eye = jnp.eye(per, dtype=w.dtype)
    out = wg[:, :, :, :, None, :] * eye[None, None, :, None, :, None]
    return out.reshape(depth, n_blocks // per, per * bs, per * bs)


def kernel(x, mem, norm_mix_g, w_in, rnn_conv_w, rnn_conv_b, lru_wa, lru_ba, lru_wx, lru_bx, lru_lambda, ml_conv_w, ml_conv_b, ml_if_b, ml_norm_g, w_branch_a, w_branch_b, w_mix_out, norm_xa_g, xa_wq, xa_wkv, xa_wo, norm_ffn_g, ffn_w_up, ffn_conv_w, ffn_conv_b, ffn_w_down, mem_norm_g, final_norm_g):
    depth, d, _ = w_in.shape
    d_ml = ml_norm_g.shape[1]
    assert d == d_ml == lru_lambda.shape[1], "kernels assume D_RNN == D_ML == D_MODEL"
    assert ML_HEADS <= V7X_SUBLANES and x.shape[1] % ML_CHUNK == 0

    o_xr, o_gr, o_q, o_k, o_v, o_o = (n * d for n in range(6))
    o_if = 6 * d
    o_ga = o_if + 2 * ML_HEADS
    o_gb = o_ga + d

    def cols(o):
        return w_in[:, :, o:o + d]

    w_main = jnp.concatenate(
        [cols(o_gr), cols(o_xr), cols(o_q), cols(o_k), cols(o_v), cols(o_o), cols(o_ga), cols(o_gb)],
        axis=-1).astype(BF16)
    w_if = jnp.swapaxes(w_in[:, :, o_if:o_ga], 1, 2)
    pad = jnp.zeros((depth, V7X_SUBLANES - ML_HEADS, d), w_in.dtype)
    w_ifT = jnp.concatenate([w_if[:, :ML_HEADS], pad, w_if[:, ML_HEADS:], pad], axis=1).astype(BF16)
    bpad = jnp.zeros((depth, V7X_SUBLANES - ML_HEADS), ml_if_b.dtype)
    b_if = jnp.concatenate([ml_if_b[:, :ML_HEADS], bpad, ml_if_b[:, ML_HEADS:], bpad], axis=1)[..., None]

    conv_w = jnp.stack([rnn_conv_w, ml_conv_w[:, :, :d], ml_conv_w[:, :, d:]], axis=1)
    conv_b = jnp.stack([rnn_conv_b, ml_conv_b[:, :d], ml_conv_b[:, d:]], axis=1)[:, :, None, :]
    w_gate = jnp.concatenate([_block_diag_groups(lru_wa), _block_diag_groups(lru_wx)], axis=-1).astype(BF16)
    b_gate = jnp.stack([lru_ba, lru_bx], axis=1)[:, :, None, :]

    def row(g):
        return g[:, None, :]

    wa, wb, wm = (w.astype(BF16) for w in (w_branch_a, w_branch_b, w_mix_out))
    wq, wkv, wo = (w.astype(BF16) for w in (xa_wq, xa_wkv, xa_wo))
    w_up, w_down = ffn_w_up.astype(BF16), ffn_w_down.astype(BF16)
    ffn_cb = row(ffn_conv_b)

    kv = _memkv(mem, mem_norm_g[None, :], wkv)
    h = _to_block_order(x)
    for l in range(depth):
        segs, ifg = _inproj(l, h, row(norm_mix_g), w_main, w_ifT, b_if, conv_w, conv_b, w_gate,
                            b_gate, row(lru_lambda))
        yb = _mlstm(l, segs, ifg, row(ml_norm_g))
        h = _merge(l, segs, yb, h, wa, wb, wm)
        h = _xattn(l, h, row(norm_xa_g), wq, kv, wo)
        h = _ffn(l, h, row(norm_ffn_g), w_up, ffn_conv_w, ffn_cb, w_down, final_norm_g[None, :],
                 final_norm=(l == depth - 1))
    return _from_block_order(h)
```

```python
import functools

import jax
import jax.numpy as jnp
from jax import lax
from jax.experimental import pallas as pl
from jax.experimental.pallas import tpu as pltpu

F32 = jnp.float32
BF16 = jnp.bfloat16

EPS = 1e-6
LRU_C = 8.0
CONV_W = 4
FFN_CONV_W = 3
RNN_BLOCK = 64
ML_HEADS = 4
ML_CHUNK = 128
XA_HEADS = 4
FFN_TILE = 512

V7X_SUBLANES = 8
V7X_LANES = 128
V7X_MXU_DIM = 256
V7X_VMEM_BYTES = 64 * 1024 * 1024
V7X_VMEM_REQUEST_CAP = 56 * 1024 * 1024

SEG_GR, SEG_XR, SEG_Q, SEG_K, SEG_V, SEG_O, SEG_GA, SEG_GB = range(8)
SLOT_YA, SLOT_Q, SLOT_K, SLOT_V, SLOT_O, SLOT_GA, SLOT_GB = range(7)
N_SEG = 8
N_SLOT = 7
IFG_ROWS = 16


def _vmem_limit(n_bytes):
    return int(min(V7X_VMEM_REQUEST_CAP, n_bytes + n_bytes // 2 + (4 << 20)))


def _params(n_grid, vmem_bytes):
    return pltpu.CompilerParams(
        dimension_semantics=("arbitrary",) * n_grid,
        vmem_limit_bytes=_vmem_limit(vmem_bytes),
    )


def _rms_norm(xf, g):
    ms = jnp.mean(xf * xf, axis=-1, keepdims=True)
    return (xf * lax.rsqrt(ms + EPS)) * g


def _softplus(z):
    return jnp.maximum(z, 0.0) + jnp.log1p(jnp.exp(-jnp.abs(z)))


def _row_chunks(n_rows, chunk, body):
    assert n_rows % chunk == 0

    def step(r, carry):
        body(pl.multiple_of(r * chunk, chunk))
        return carry

    lax.fori_loop(0, n_rows // chunk, step, 0)


TOK_BLOCK = ML_CHUNK
TOK_SEG = TOK_BLOCK // V7X_SUBLANES


def _to_block_order(t):
    b, s, d = t.shape
    t = t.reshape(b, s // TOK_BLOCK, V7X_SUBLANES, TOK_SEG, d)
    return jnp.swapaxes(t, 2, 3).reshape(b, s, d)


def _from_block_order(t):
    b, s, d = t.shape
    t = t.reshape(b, s // TOK_BLOCK, TOK_SEG, V7X_SUBLANES, d)
    return jnp.swapaxes(t, 2, 3).reshape(b, s, d)


def _time_of_row(row):
    return (row % V7X_SUBLANES) * TOK_SEG + row // V7X_SUBLANES


def _row_of_time(tau):
    return (tau % TOK_SEG) * V7X_SUBLANES + tau // TOK_SEG


def _halo_vregs(prev_tail, cur_tail):
    sub = lax.broadcasted_iota(jnp.int32, (V7X_SUBLANES, cur_tail.shape[1]), 0)
    out = []
    for i in range(cur_tail.shape[0] // V7X_SUBLANES):
        rows = slice(i * V7X_SUBLANES, (i + 1) * V7X_SUBLANES)
        mixed = jnp.where(sub == V7X_SUBLANES - 1, prev_tail[rows], cur_tail[rows])
        out.append(pltpu.roll(mixed, 1, 0))
    return out


def _causal_conv_run(prev_tail, cur, cw, cb):
    taps = cw.shape[0]
    n = cur.shape[0]
    tail = (taps - 1) * V7X_SUBLANES
    ext = jnp.concatenate(_halo_vregs(prev_tail, cur[n - tail:]) + [cur], axis=0)
    out = cb
    for k in range(taps):
        out = out + ext[k * V7X_SUBLANES:k * V7X_SUBLANES + n] * cw[k:k + 1]
    return out


def _memkv_kernel(mem_ref, g_ref, w_ref, o_ref):
    memn = _rms_norm(mem_ref[0], g_ref[...]).astype(BF16)
    o_ref[0, 0] = jnp.dot(memn, w_ref[0], preferred_element_type=F32).astype(BF16)


def _memkv(mem, mem_g, wkv):
    depth, d, n = wkv.shape
    b, m, _ = mem.shape
    vmem = 2 * (m * d * 4 + d * n * 2 + m * n * 2) + m * n * 4
    return pl.pallas_call(
        _memkv_kernel,
        grid=(depth, b),
        in_specs=[
            pl.BlockSpec((1, m, d), lambda l, i: (i, 0, 0)),
            pl.BlockSpec((1, d), lambda l, i: (0, 0)),
            pl.BlockSpec((1, d, n), lambda l, i: (l, 0, 0)),
        ],
        out_specs=pl.BlockSpec((1, 1, m, n), lambda l, i: (l, i, 0, 0)),
        out_shape=jax.ShapeDtypeStruct((depth, b, m, n), BF16),
        compiler_params=_params(2, vmem),
        name="memkv",
    )(mem, mem_g, wkv)


def _inproj_kernel(x_ref, g_ref, w_ref, wif_ref, bif_ref, cw_ref, cb_ref, wg_ref, bg_ref,
                   lam_ref, seg_ref, ifg_ref,
                   xn_scr, pc_scr, gelu_scr, xc_scr, xcb_scr, a_scr, u_scr,
                   carry_scr, hlast_scr, *, tm, rc, n_blk, n_steps):
    m = pl.program_id(1)
    seg_done = (m + N_SEG - 1) % N_SEG
    c = xn_scr.shape[1]
    halo = (CONV_W - 1) * V7X_SUBLANES
    gw = V7X_MXU_DIM
    chunks = range(0, tm, rc)
    runs = [(r0, slice(l0, l0 + V7X_LANES))
            for r0 in range(0, tm, TOK_BLOCK) for l0 in range(0, c, V7X_LANES)]

    def norm_and_gates():
        xn = _rms_norm(x_ref[0], g_ref[0]).astype(BF16)
        xn_scr[...] = xn
        ifg = lax.dot_general(wif_ref[0], xn, (((1,), (1,)), ((), ())),
                              preferred_element_type=F32)
        ifg_ref[0] = ifg + bif_ref[0]

    bm = tm // n_blk

    def project_rows(par, b):
        pc_scr[par, halo + b * bm:halo + (b + 1) * bm, :] = jnp.dot(
            xn_scr[b * bm:(b + 1) * bm, :], w_ref[0, 0], preferred_element_type=F32)

    def project(par):
        for b in range(n_blk):
            project_rows(par, b)

    def interleave(par_next, items):
        n = len(items)
        for b in range(n_blk):
            project_rows(par_next, b)
            for item in items[b * n // n_blk:(b + 1) * n // n_blk]:
                item()

    def conv_run(par, ci, r0, lanes):
        base = halo + r0
        return _causal_conv_run(pc_scr[par, base - halo:base, lanes],
                                pc_scr[par, base:base + TOK_BLOCK, lanes],
                                cw_ref[0, ci][:, lanes], cb_ref[0, ci][:, lanes])

    def load_carry(par, ci):
        pc_scr[par, 0:halo, :] = carry_scr[ci]

    def save_carry(par, ci):
        carry_scr[ci] = pc_scr[par, tm:tm + halo, :]

    def gate_branch(par):
        def item(r0):
            gelu_scr[r0:r0 + rc, :] = jax.nn.gelu(pc_scr[par, halo + r0:halo + r0 + rc, :])
        return [functools.partial(item, r0) for r0 in chunks]

    def rg_lru(par):
        items = [functools.partial(load_carry, par, 0)]

        def conv_item(r0, lanes):
            xc = conv_run(par, 0, r0, lanes)
            xc_scr[r0:r0 + TOK_BLOCK, lanes] = xc
            xcb_scr[r0:r0 + TOK_BLOCK, lanes] = xc.astype(BF16)
        items += [functools.partial(conv_item, r0, lanes) for r0, lanes in runs]
        items.append(functools.partial(save_carry, par, 0))

        state = {}

        def gate_dot(g):
            sl = slice(g * gw, (g + 1) * gw)
            state["gt"] = jnp.dot(xcb_scr[:, sl], wg_ref[0, g], preferred_element_type=F32)
            state["c_row"] = -LRU_C * _softplus(-lam_ref[0][:, sl])

        def gate_item(g, r0):
            sl = slice(g * gw, (g + 1) * gw)
            gt = state["gt"]
            r = jax.nn.sigmoid(gt[r0:r0 + rc, :gw] + bg_ref[0, 0][:, sl])
            ig = jax.nn.sigmoid(gt[r0:r0 + rc, gw:] + bg_ref[0, 1][:, sl])
            log_a = state["c_row"] * r
            a = jnp.exp(log_a)
            mult = jnp.sqrt(-jnp.tanh(log_a) * (a * a + 1.0))
            a_scr[r0:r0 + rc, sl] = a
            u_scr[r0:r0 + rc, sl] = mult * (ig * xc_scr[r0:r0 + rc, sl])

        for g in range(c // gw):
            items.append(functools.partial(gate_dot, g))
            items += [functools.partial(gate_item, g, r0) for r0 in chunks]

        def scan_item(r0, lanes):
            nsub = V7X_SUBLANES
            sub = lax.broadcasted_iota(jnp.int32, (nsub, V7X_LANES), 0)
            key = ("h", lanes.start)
            hprev = hlast_scr[:, lanes] if r0 == 0 else state[key]
            local, decay = [], []
            for r in range(TOK_SEG):
                rows = slice(r0 + r * nsub, r0 + (r + 1) * nsub)
                a = a_scr[rows, lanes]
                u = u_scr[rows, lanes]
                local.append(u if r == 0 else a * local[-1] + u)
                decay.append(a if r == 0 else a * decay[-1])
            ea, eu = decay[-1], local[-1]
            for k in (1, 2, 4):
                keep = sub >= k
                eu = jnp.where(keep, eu + ea * pltpu.roll(eu, k, 0), eu)
                ea = jnp.where(keep, ea * pltpu.roll(ea, k, 0), ea)
            ends = eu + ea * hprev
            h_in = pltpu.roll(jnp.where(sub == nsub - 1, hprev, ends), 1, 0)
            for r in range(TOK_SEG):
                rows = slice(r0 + r * nsub, r0 + (r + 1) * nsub)
                u_scr[rows, lanes] = local[r] + decay[r] * h_in
            hnew = jnp.broadcast_to(ends[nsub - 1:nsub, :], (nsub, V7X_LANES))
            state[key] = hnew
            if r0 + TOK_BLOCK == tm:
                hlast_scr[:, lanes] = hnew

        def out_item(r0):
            ya = gelu_scr[r0:r0 + rc, :] * u_scr[r0:r0 + rc, :]
            seg_ref[0, 0, r0:r0 + rc, :] = ya.astype(BF16)

        items += [functools.partial(scan_item, r0, lanes) for r0, lanes in runs]
        items += [functools.partial(out_item, r0) for r0 in chunks]
        return items

    def conv_silu(ci, par):
        def item(r0, lanes):
            seg_ref[0, 0, r0:r0 + TOK_BLOCK, lanes] = jax.nn.silu(
                conv_run(par, ci, r0, lanes)).astype(BF16)
        return ([functools.partial(load_carry, par, ci)]
                + [functools.partial(item, r0, lanes) for r0, lanes in runs]
                + [functools.partial(save_carry, par, ci)])

    def plain(par):
        def item(r0):
            seg_ref[0, 0, r0:r0 + rc, :] = pc_scr[par, halo + r0:halo + r0 + rc, :].astype(BF16)
        return [functools.partial(item, r0) for r0 in chunks]

    epilogues = {
        SEG_GR: gate_branch,
        SEG_XR: rg_lru,
        SEG_Q: functools.partial(conv_silu, 1),
        SEG_K: functools.partial(conv_silu, 2),
    }

    @pl.when(m == 0)
    def _():
        carry_scr[...] = jnp.zeros_like(carry_scr)
        hlast_scr[...] = jnp.zeros_like(hlast_scr)
        norm_and_gates()
        project(0)

    for j in range(N_SEG):
        @pl.when((m > 0) & (seg_done == j))
        def _(j=j):
            par = j % 2
            items = epilogues.get(j, plain)(par)
            if j + 1 < N_SEG:
                interleave(1 - par, items)
            else:
                for item in items:
                    item()

                @pl.when(m < n_steps)
                def _():
                    norm_and_gates()
                    project(1 - par)


def _inproj(layer, h, norm_g, w_main, w_ifT, b_if, conv_w, conv_b, w_gate, b_gate, lam, *, tm=512,
            rc=32, n_blk=4):
    b, s, d = h.shape
    n_groups = d // V7X_MXU_DIM
    halo = (CONV_W - 1) * V7X_SUBLANES
    assert tm % TOK_BLOCK == 0
    n_tiles = s // tm
    n_steps = n_tiles * N_SEG
    kern = functools.partial(_inproj_kernel, tm=tm, rc=rc, n_blk=n_blk, n_steps=n_steps)
    vmem = (2 * tm * d * 4 + 2 * d * d * 2 + 2 * tm * d * 2 + 2 * IFG_ROWS * tm * 4
            + tm * d * 2 + 2 * (tm + halo) * d * 4 + 4 * tm * d * 4 + tm * d * 2
            + 2 * n_groups * V7X_MXU_DIM * 2 * V7X_MXU_DIM * 2 + 3 * tm * d * 4)
    l = layer

    def proj_tile(m):
        return jnp.minimum(m // N_SEG, n_tiles - 1)

    def done_tile(m):
        return jnp.maximum(m - 1, 0) // N_SEG

    def done_slot(m):
        return jnp.maximum(jnp.maximum(m - 1, 0) % N_SEG - 1, 0)

    return pl.pallas_call(
        kern,
        grid=(b, n_steps + 1),
        in_specs=[
            pl.BlockSpec((1, tm, d), lambda bi, m: (bi, proj_tile(m), 0)),
            pl.BlockSpec((1, 1, d), lambda bi, m: (l, 0, 0)),
            pl.BlockSpec((1, 1, d, d), lambda bi, m: (l, m % N_SEG, 0, 0)),
            pl.BlockSpec((1, IFG_ROWS, d), lambda bi, m: (l, 0, 0)),
            pl.BlockSpec((1, IFG_ROWS, 1), lambda bi, m: (l, 0, 0)),
            pl.BlockSpec((1, 3, CONV_W, d), lambda bi, m: (l, 0, 0, 0)),
            pl.BlockSpec((1, 3, 1, d), lambda bi, m: (l, 0, 0, 0)),
            pl.BlockSpec((1, n_groups, V7X_MXU_DIM, 2 * V7X_MXU_DIM), lambda bi, m: (l, 0, 0, 0)),
            pl.BlockSpec((1, 2, 1, d), lambda bi, m: (l, 0, 0, 0)),
            pl.BlockSpec((1, 1, d), lambda bi, m: (l, 0, 0)),
        ],
        out_specs=[
            pl.BlockSpec((1, 1, tm, d), lambda bi, m: (done_slot(m), bi, done_tile(m), 0)),
            pl.BlockSpec((1, IFG_ROWS, tm), lambda bi, m: (bi, 0, proj_tile(m))),
        ],
        out_shape=[
            jax.ShapeDtypeStruct((N_SLOT, b, s, d), BF16),
            jax.ShapeDtypeStruct((b, IFG_ROWS, s), F32),
        ],
        scratch_shapes=[
            pltpu.VMEM((tm, d), BF16),
            pltpu.VMEM((2, tm + halo, d), F32),
            pltpu.VMEM((tm, d), F32),
            pltpu.VMEM((tm, d), F32),
            pltpu.VMEM((tm, d), BF16),
            pltpu.VMEM((tm, d), F32),
            pltpu.VMEM((tm, d), F32),
            pltpu.VMEM((3, halo, d), F32),
            pltpu.VMEM((V7X_SUBLANES, d), F32),
        ],
        compiler_params=_params(2, vmem),
        name=f"inproj{layer}",
    )(h, norm_g, w_main, w_ifT, b_if, conv_w, conv_b, w_gate, b_gate, lam)


def _lane_scan(x, lane, op, fill):
    n = x.shape[-1]
    k = 1
    while k < n:
        x = op(x, jnp.where(lane >= k, pltpu.roll(x, k, 1), fill))
        k *= 2
    return x


_DONE = object()


def _mlstm_kernel(q_ref, k_ref, v_ref, o_ref, ifg_ref, ng_ref, y_ref, c_scr, n_scr, m_scr):
    @pl.when(pl.program_id(1) == 0)
    def _():
        c_scr[...] = jnp.zeros_like(c_scr)
        n_scr[...] = jnp.zeros_like(n_scr)
        m_scr[...] = jnp.zeros_like(m_scr)

    nb = y_ref.shape[0]
    gates = [_mlstm_gates(bb, ifg_ref, m_scr) for bb in range(nb)]
    heads = [_mlstm_head(bb, hd, gates[bb], q_ref, k_ref, v_ref, o_ref, ng_ref, y_ref, c_scr, n_scr)
             for hd in range(ML_HEADS) for bb in range(nb)]
    while heads:
        heads = [h for h in heads if next(h, _DONE) is not _DONE]


def _mlstm_gates(bb, ifg_ref, m_scr):
    L = ML_CHUNK
    rows = V7X_SUBLANES

    ifg = ifg_ref[bb]
    li = ifg[0:rows]
    lf = -_softplus(-ifg[rows:2 * rows])
    src = lax.broadcasted_iota(jnp.int32, (L, L), 0)
    dst = lax.broadcasted_iota(jnp.int32, (L, L), 1)
    not_after = (_time_of_row(src) <= _time_of_row(dst)).astype(F32)
    bcum = jnp.dot(lf, not_after, precision=lax.Precision.HIGHEST,
                   preferred_element_type=F32)
    m_st = m_scr[bb]
    g_tot = jnp.broadcast_to(bcum[:, L - 1:L], (rows, L))
    r1 = li - bcum
    u = g_tot - bcum + li
    m_next = jnp.maximum(g_tot + m_st, jnp.max(u, axis=-1, keepdims=True))
    decay = jnp.exp(g_tot + m_st - m_next)
    uexp = jnp.exp(u - m_next)
    m_scr[bb] = m_next

    stacked = jnp.concatenate(
        [bcum, m_st, uexp, jnp.zeros((L - 3 * rows, L), F32)], axis=0)
    cols = stacked.T
    return cols, r1, decay


def _mlstm_head(bb, hd, gates, q_ref, k_ref, v_ref, o_ref, ng_ref, y_ref, c_scr, n_scr):
    L = ML_CHUNK
    dh = c_scr.shape[2]
    rows = V7X_SUBLANES
    cols, r1, decay = gates
    tt = lax.broadcasted_iota(jnp.int32, (L, L), 0)
    ss = lax.broadcasted_iota(jnp.int32, (L, L), 1)
    causal = _time_of_row(ss) <= _time_of_row(tt)

    sl = slice(hd * dh, (hd + 1) * dh)
    qh = q_ref[0, bb, :, sl]
    kh = k_ref[0, bb, :, sl] * jnp.asarray(dh ** -0.5, BF16)
    vh = v_ref[0, bb, :, sl]
    b_col = cols[:, hd:hd + 1]
    a_inter = b_col + cols[:, rows + hd:rows + hd + 1]
    uec = cols[:, 2 * rows + hd:2 * rows + hd + 1]
    dec = decay[hd:hd + 1, 0:1]

    dmat = jnp.where(causal, b_col + r1[hd:hd + 1, :], -jnp.inf)
    m_t = jnp.maximum(a_inter, jnp.max(dmat, axis=-1, keepdims=True))
    qk = lax.dot_general(qh, kh, (((1,), (1,)), ((), ())), preferred_element_type=F32)
    c_old = c_scr[bb, hd]
    n_old = n_scr[bb, hd:hd + 1, :]
    qc = jnp.dot(qh, c_old.astype(BF16), preferred_element_type=F32)
    qn = jnp.sum(qh.astype(F32) * n_old, axis=-1, keepdims=True)
    yield
    sc = qk * jnp.exp(dmat - m_t)
    wic = jnp.exp(a_inter - m_t)
    enc = jnp.exp(-m_t)
    num = wic * qc + jnp.dot(sc.astype(BF16), vh, preferred_element_type=F32)
    den = wic * qn + jnp.sum(sc, axis=-1, keepdims=True)
    wk = kh.astype(F32) * uec
    c_scr[bb, hd] = dec * c_old + lax.dot_general(
        wk.astype(BF16), vh, (((0,), (0,)), ((), ())), preferred_element_type=F32)
    n_scr[bb, hd:hd + 1, :] = dec * n_old + jnp.sum(wk, axis=0, keepdims=True)
    yield
    hh = num * (1.0 / jnp.maximum(jnp.abs(den), enc))
    mu = jnp.mean(hh, axis=-1, keepdims=True)
    yield
    cen = hh - mu
    var = jnp.mean(cen * cen, axis=-1, keepdims=True)
    yield
    hn = (cen * lax.rsqrt(var + EPS)) * ng_ref[0][:, sl]
    gate = jax.nn.sigmoid(o_ref[0, bb, :, sl].astype(F32))
    y_ref[bb, :, sl] = (gate * hn).astype(BF16)


def _mlstm(layer, segs, ifg, ml_norm_g, *, nb=2):
    _, b, s, d = segs.shape
    L = ML_CHUNK
    dh = d // ML_HEADS
    l = layer

    def seg_spec(slot):
        return pl.BlockSpec((1, nb, L, d), lambda bi, ci: (slot, bi, ci, 0))

    vmem = nb * (2 * (4 * L * d * 2 + IFG_ROWS * L * 4 + L * d * 2) + ML_HEADS * dh * dh * 4
                 + 16 * L * d * 4)
    return pl.pallas_call(
        _mlstm_kernel,
        grid=(b // nb, s // L),
        in_specs=[
            seg_spec(SLOT_Q), seg_spec(SLOT_K), seg_spec(SLOT_V), seg_spec(SLOT_O),
            pl.BlockSpec((nb, IFG_ROWS, L), lambda bi, ci: (bi, 0, ci)),
            pl.BlockSpec((1, 1, d), lambda bi, ci: (l, 0, 0)),
        ],
        out_specs=pl.BlockSpec((nb, L, d), lambda bi, ci: (bi, ci, 0)),
        out_shape=jax.ShapeDtypeStruct((b, s, d), BF16),
        scratch_shapes=[
            pltpu.VMEM((nb, ML_HEADS, dh, dh), F32),
            pltpu.VMEM((nb, V7X_SUBLANES, dh), F32),
            pltpu.VMEM((nb, V7X_SUBLANES, L), F32),
        ],
        compiler_params=_params(2, vmem),
        name=f"mlstm{layer}",
    )(segs, segs, segs, segs, ifg, ml_norm_g)


def _merge_kernel(ya_ref, yb_ref, ga_ref, gb_ref, h_ref, wa_ref, wb_ref, wm_ref, o_ref):
    pa = jnp.dot(ya_ref[0, 0], wa_ref[0], preferred_element_type=F32)
    pb = jnp.dot(yb_ref[0], wb_ref[0], preferred_element_type=F32)
    y = jax.nn.sigmoid(ga_ref[0, 0].astype(F32)) * pa + jax.nn.sigmoid(gb_ref[0, 0].astype(F32)) * pb
    o_ref[0] = h_ref[0] + jnp.dot(y.astype(BF16), wm_ref[0], preferred_element_type=F32)


def _merge(layer, segs, yb, h, wa, wb, wm, *, tm=512):
    b, s, d = h.shape
    l = layer

    def seg_spec(slot):
        return pl.BlockSpec((1, 1, tm, d), lambda bi, i: (slot, bi, i, 0))

    w_spec = pl.BlockSpec((1, d, d), lambda bi, i: (l, 0, 0))
    row_spec = pl.BlockSpec((1, tm, d), lambda bi, i: (bi, i, 0))
    vmem = 2 * (4 * tm * d * 2 + 2 * tm * d * 4 + 3 * d * d * 2) + 4 * tm * d * 4
    return pl.pallas_call(
        _merge_kernel,
        grid=(b, s // tm),
        in_specs=[seg_spec(SLOT_YA), row_spec, seg_spec(SLOT_GA), seg_spec(SLOT_GB), row_spec,
                  w_spec, w_spec, w_spec],
        out_specs=row_spec,
        out_shape=jax.ShapeDtypeStruct((b, s, d), F32),
        compiler_params=_params(2, vmem),
        name=f"merge{layer}",
    )(segs, yb, segs, segs, h, wa, wb, wm)


def _xattn_kernel(h_ref, g_ref, wq_ref, kv_ref, wo_ref, o_ref, att_scr):
    d = h_ref.shape[2]
    dh = d // XA_HEADS
    hres = h_ref[0]
    xn = _rms_norm(hres, g_ref[0]).astype(BF16)
    q = jnp.dot(xn, wq_ref[0], preferred_element_type=F32).astype(BF16)
    scale = dh ** -0.5
    for hd in range(XA_HEADS):
        sl = slice(hd * dh, (hd + 1) * dh)
        kh = kv_ref[0, 0, :, sl]
        vh = kv_ref[0, 0, :, d + hd * dh:d + (hd + 1) * dh]
        sc = lax.dot_general(q[:, sl], kh, (((1,), (1,)), ((), ())),
                             preferred_element_type=F32) * scale
        e = jnp.exp(sc - jnp.max(sc, axis=-1, keepdims=True))
        p = e * (1.0 / jnp.sum(e, axis=-1, keepdims=True))
        att_scr[:, sl] = jnp.dot(p.astype(BF16), vh, preferred_element_type=F32).astype(BF16)
    o_ref[0] = hres + jnp.dot(att_scr[...], wo_ref[0], preferred_element_type=F32)


def _xattn(layer, h, norm_g, wq, kv, wo, *, tm=512):
    b, s, d = h.shape
    m = kv.shape[2]
    l = layer
    row_spec = pl.BlockSpec((1, tm, d), lambda bi, i: (bi, i, 0))
    w_spec = pl.BlockSpec((1, d, d), lambda bi, i: (l, 0, 0))
    vmem = 2 * (2 * tm * d * 4 + 2 * d * d * 2 + m * 2 * d * 2) + tm * d * 2 + 6 * tm * d * 4
    return pl.pallas_call(
        _xattn_kernel,
        grid=(b, s // tm),
        in_specs=[
            row_spec,
            pl.BlockSpec((1, 1, d), lambda bi, i: (l, 0, 0)),
            w_spec,
            pl.BlockSpec((1, 1, m, 2 * d), lambda bi, i: (l, bi, 0, 0)),
            w_spec,
        ],
        out_specs=row_spec,
        out_shape=jax.ShapeDtypeStruct((b, s, d), F32),
        scratch_shapes=[pltpu.VMEM((tm, d), BF16)],
        compiler_params=_params(2, vmem),
        name=f"xattn{layer}",
    )(h, norm_g, wq, kv, wo)


def _ffn_kernel(x_ref, g_ref, wg_ref, wu_ref, cwg_ref, cwu_ref, cbg_ref, cbu_ref, wd_ref, fg_ref,
                o_ref, xn_scr, acc_scr, pc_scr, act_scr, carry_scr, *, tm, rc, n_j, n_blk,
                final_norm):
    i = pl.program_id(1)
    j = pl.program_id(2)
    halo = (FFN_CONV_W - 1) * V7X_SUBLANES
    tf = act_scr.shape[1]

    @pl.when(j == 0)
    def _():
        xn_scr[...] = _rms_norm(x_ref[0], g_ref[0]).astype(BF16)
        acc_scr[...] = jnp.zeros_like(acc_scr)

    @pl.when(i == 0)
    def _():
        carry_scr[j] = jnp.zeros(carry_scr.shape[1:], F32)

    for half in range(2):
        pc_scr[half, 0:halo, :] = carry_scr[j, half]

    def conv_run(half, cw_ref, cb_ref, r0, lanes):
        base = halo + r0
        return _causal_conv_run(pc_scr[half, base - halo:base, lanes],
                                pc_scr[half, base:base + TOK_BLOCK, lanes],
                                cw_ref[0][:, lanes], cb_ref[0][:, lanes])

    bm = tm // n_blk

    def up(b):
        for half, w_ref in enumerate((wg_ref, wu_ref)):
            pc_scr[half, halo + b * bm:halo + (b + 1) * bm, :] = jnp.dot(
                xn_scr[b * bm:(b + 1) * bm, :], w_ref[0, 0], preferred_element_type=F32)

    def epilogue(b):
        for r0 in range(b * bm, (b + 1) * bm, TOK_BLOCK):
            for l0 in range(0, tf, V7X_LANES):
                lanes = slice(l0, l0 + V7X_LANES)
                gpre = conv_run(0, cwg_ref, cbg_ref, r0, lanes)
                upre = conv_run(1, cwu_ref, cbu_ref, r0, lanes)
                act_scr[r0:r0 + TOK_BLOCK, lanes] = (jax.nn.gelu(gpre) * upre).astype(BF16)

    def down(b):
        rows = slice(b * bm, (b + 1) * bm)
        acc_scr[rows, :] += jnp.dot(act_scr[rows, :], wd_ref[0], preferred_element_type=F32)

    up(0)
    for b in range(n_blk):
        if b + 1 < n_blk:
            up(b + 1)
        epilogue(b)
        down(b)

    for half in range(2):
        carry_scr[j, half] = pc_scr[half, tm:tm + halo, :]

    @pl.when(j == n_j - 1)
    def _():
        def out_body(r0):
            res = x_ref[0, pl.ds(r0, rc), :] + acc_scr[pl.ds(r0, rc), :]
            if final_norm:
                res = _rms_norm(res, fg_ref[...])
            o_ref[0, pl.ds(r0, rc), :] = res
        _row_chunks(tm, rc, out_body)


def _ffn(layer, h, norm_g, w_up, conv_w, conv_b, w_down, final_g, *, final_norm, tm=1024, rc=32,
         n_blk=4):
    b, s, d = h.shape
    tf = w_up.shape[3]
    n_j = w_up.shape[1] // 2
    halo = (FFN_CONV_W - 1) * V7X_SUBLANES
    assert (tm // n_blk) % TOK_BLOCK == 0
    l = layer
    kern = functools.partial(_ffn_kernel, tm=tm, rc=rc, n_j=n_j, n_blk=n_blk,
                             final_norm=final_norm)
    row_spec = pl.BlockSpec((1, tm, d), lambda bi, i, j: (bi, i, 0))
    vmem = (4 * tm * d * 4 + 2 * (2 * d * tf * 2 + tf * d * 2) + tm * d * 2 + tm * d * 4
            + 2 * (tm + halo) * tf * 4 + tm * tf * 2 + n_j * 2 * halo * tf * 4 + 2 * tm * tf * 4)
    return pl.pallas_call(
        kern,
        grid=(b, s // tm, n_j),
        in_specs=[
            row_spec,
            pl.BlockSpec((1, 1, d), lambda bi, i, j: (l, 0, 0)),
            pl.BlockSpec((1, 1, d, tf), lambda bi, i, j: (l, j, 0, 0)),
            pl.BlockSpec((1, 1, d, tf), lambda bi, i, j: (l, n_j + j, 0, 0)),
            pl.BlockSpec((1, FFN_CONV_W, tf), lambda bi, i, j: (l, 0, j)),
            pl.BlockSpec((1, FFN_CONV_W, tf), lambda bi, i, j: (l, 0, n_j + j)),
            pl.BlockSpec((1, 1, tf), lambda bi, i, j: (l, 0, j)),
            pl.BlockSpec((1, 1, tf), lambda bi, i, j: (l, 0, n_j + j)),
            pl.BlockSpec((1, tf, d), lambda bi, i, j: (l, j, 0)),
            pl.BlockSpec((1, d), lambda bi, i, j: (0, 0)),
        ],
        out_specs=row_spec,
        out_shape=jax.ShapeDtypeStruct((b, s, d), F32),
        scratch_shapes=[
            pltpu.VMEM((tm, d), BF16),
            pltpu.VMEM((tm, d), F32),
            pltpu.VMEM((2, tm + halo, tf), F32),
            pltpu.VMEM((tm, tf), BF16),
            pltpu.VMEM((n_j, 2, halo, tf), F32),
        ],
        compiler_params=_params(3, vmem),
        name=f"ffn{layer}",
    )(h, norm_g, w_up, w_up, conv_w, conv_w, conv_b, conv_b, w_down, final_g)


def _block_diag_groups(w):
    depth, n_blocks, bs, _ = w.shape
    per = V7X_MXU_DIM // bs
    wg = w.reshape(depth, n_blocks // per, per, bs, bs)
    eye = jnp.eye(per, dtype=w.dtype)
    out = wg[:, :, :, :, None, :] * eye[None, None, :, None, :, None]
    return out.reshape(depth, n_blocks // per, per * bs, per * bs)


def kernel(x, mem, norm_mix_g, w_in, rnn_conv_w, rnn_conv_b, lru_wa, lru_ba, lru_wx, lru_bx, lru_lambda, ml_conv_w, ml_conv_b, ml_if_b, ml_norm_g, w_branch_a, w_branch_b, w_mix_out, norm_xa_g, xa_wq, xa_wkv, xa_wo, norm_ffn_g, ffn_w_up, ffn_conv_w, ffn_conv_b, ffn_w_down, mem_norm_g, final_norm_g):
    depth, d, _ = w_in.shape
    d_ml = ml_norm_g.shape[1]
    assert d == d_ml == lru_lambda.shape[1], "kernels assume D_RNN == D_ML == D_MODEL"
    assert ML_HEADS <= V7X_SUBLANES and x.shape[1] % ML_CHUNK == 0

    o_xr, o_gr, o_q, o_k, o_v, o_o = (n * d for n in range(6))
    o_if = 6 * d
    o_ga = o_if + 2 * ML_HEADS
    o_gb = o_ga + d

    def cols(o):
        return w_in[:, :, o:o + d]

    w_main = jnp.stack(
        [cols(o_gr), cols(o_xr), cols(o_q), cols(o_k), cols(o_v), cols(o_o), cols(o_ga), cols(o_gb)],
        axis=1).astype(BF16)
    w_if = jnp.swapaxes(w_in[:, :, o_if:o_ga], 1, 2)
    pad = jnp.zeros((depth, V7X_SUBLANES - ML_HEADS, d), w_in.dtype)
    w_ifT = jnp.concatenate([w_if[:, :ML_HEADS], pad, w_if[:, ML_HEADS:], pad], axis=1).astype(BF16)
    bpad = jnp.zeros((depth, V7X_SUBLANES - ML_HEADS), ml_if_b.dtype)
    b_if = jnp.concatenate([ml_if_b[:, :ML_HEADS], bpad, ml_if_b[:, ML_HEADS:], bpad], axis=1)[..., None]

    conv_w = jnp.stack([rnn_conv_w, ml_conv_w[:, :, :d], ml_conv_w[:, :, d:]], axis=1)
    conv_b = jnp.stack([rnn_conv_b, ml_conv_b[:, :d], ml_conv_b[:, d:]], axis=1)[:, :, None, :]
    w_gate = jnp.concatenate([_block_diag_groups(lru_wa), _block_diag_groups(lru_wx)], axis=-1).astype(BF16)
    b_gate = jnp.stack([lru_ba, lru_bx], axis=1)[:, :, None, :]

    def row(g):
        return g[:, None, :]

    wa, wb, wm = (w.astype(BF16) for w in (w_branch_a, w_branch_b, w_mix_out))
    wq, wkv, wo = (w.astype(BF16) for w in (xa_wq, xa_wkv, xa_wo))
    w_down = ffn_w_down.astype(BF16)
    w_up = ffn_w_up.reshape(depth, d, -1, FFN_TILE)
    w_up = jnp.swapaxes(w_up, 1, 2).astype(BF16)
    ffn_cb = row(ffn_conv_b)

    kv = _memkv(mem, mem_norm_g[None, :], wkv)
    h = _to_block_order(x)
    for l in range(depth):
        segs, ifg = _inproj(l, h, row(norm_mix_g), w_main, w_ifT, b_if, conv_w, conv_b, w_gate,
                            b_gate, row(lru_lambda))
        yb = _mlstm(l, segs, ifg, row(ml_norm_g))
        h = _merge(l, segs, yb, h, wa, wb, wm)
        h = _xattn(l, h, row(norm_xa_g), wq, kv, wo)
        h = _ffn(l, h, row(norm_ffn_g), w_up, ffn_conv_w, ffn_cb, w_down, final_norm_g[None, :],
                 final_norm=(l == depth - 1))
    return _from_block_order(h)
```

```python
import functools

import jax
import jax.numpy as jnp
from jax import lax
from jax.experimental import pallas as pl
from jax.experimental.pallas import tpu as pltpu

F32 = jnp.float32
BF16 = jnp.bfloat16

EPS = 1e-6
LRU_C = 8.0
CONV_W = 4
FFN_CONV_W = 3
RNN_BLOCK = 64
ML_HEADS = 4
ML_CHUNK = 128
XA_HEADS = 4
FFN_TILE = 1024

V7X_SUBLANES = 8
V7X_LANES = 128
V7X_MXU_DIM = 256
V7X_VMEM_BYTES = 64 * 1024 * 1024
V7X_VMEM_REQUEST_CAP = 56 * 1024 * 1024

SEG_GR, SEG_XR, SEG_Q, SEG_K, SEG_V, SEG_O, SEG_GA, SEG_GB = range(8)
SLOT_Q, SLOT_K, SLOT_V, SLOT_O, SLOT_GA, SLOT_GB = range(6)
N_SEG = 8
N_SLOT = 6
FIRST_STORED_SEG = SEG_Q
IFG_ROWS = 16


def _vmem_limit(n_bytes):
    return int(min(V7X_VMEM_REQUEST_CAP, n_bytes + n_bytes // 2 + (4 << 20)))


def _params(n_grid, vmem_bytes):
    return pltpu.CompilerParams(
        dimension_semantics=("arbitrary",) * n_grid,
        vmem_limit_bytes=_vmem_limit(vmem_bytes),
    )


def _rms_norm(xf, g):
    ms = jnp.mean(xf * xf, axis=-1, keepdims=True)
    return (xf * lax.rsqrt(ms + EPS)) * g


def _softplus(z):
    return jnp.maximum(z, 0.0) + jnp.log1p(jnp.exp(-jnp.abs(z)))


def _row_chunks(n_rows, chunk, body):
    assert n_rows % chunk == 0

    def step(r, carry):
        body(pl.multiple_of(r * chunk, chunk))
        return carry

    lax.fori_loop(0, n_rows // chunk, step, 0)


TOK_BLOCK = ML_CHUNK
TOK_SEG = TOK_BLOCK // V7X_SUBLANES


def _to_block_order(t):
    b, s, d = t.shape
    t = t.reshape(b, s // TOK_BLOCK, V7X_SUBLANES, TOK_SEG, d)
    return jnp.swapaxes(t, 2, 3).reshape(b, s, d)


def _from_block_order(t):
    b, s, d = t.shape
    t = t.reshape(b, s // TOK_BLOCK, TOK_SEG, V7X_SUBLANES, d)
    return jnp.swapaxes(t, 2, 3).reshape(b, s, d)


def _time_of_row(row):
    return (row % V7X_SUBLANES) * TOK_SEG + row // V7X_SUBLANES


def _row_of_time(tau):
    return (tau % TOK_SEG) * V7X_SUBLANES + tau // TOK_SEG


def _halo_vregs(prev_tail, cur_tail):
    sub = lax.broadcasted_iota(jnp.int32, (V7X_SUBLANES, cur_tail.shape[1]), 0)
    out = []
    for i in range(cur_tail.shape[0] // V7X_SUBLANES):
        rows = slice(i * V7X_SUBLANES, (i + 1) * V7X_SUBLANES)
        mixed = jnp.where(sub == V7X_SUBLANES - 1, prev_tail[rows], cur_tail[rows])
        out.append(pltpu.roll(mixed, 1, 0))
    return out


def _causal_conv_run(prev_tail, cur, cw, cb):
    taps = cw.shape[0]
    n = cur.shape[0]
    tail = (taps - 1) * V7X_SUBLANES
    ext = jnp.concatenate(_halo_vregs(prev_tail, cur[n - tail:]) + [cur], axis=0)
    out = cb
    for k in range(taps):
        out = out + ext[k * V7X_SUBLANES:k * V7X_SUBLANES + n] * cw[k:k + 1]
    return out


def _memkv_kernel(mem_ref, g_ref, w_ref, o_ref):
    memn = _rms_norm(mem_ref[0], g_ref[...]).astype(BF16)
    o_ref[0, 0] = jnp.dot(memn, w_ref[0], preferred_element_type=F32).astype(BF16)


def _memkv(mem, mem_g, wkv):
    depth, d, n = wkv.shape
    b, m, _ = mem.shape
    vmem = 2 * (m * d * 4 + d * n * 2 + m * n * 2) + m * n * 4
    return pl.pallas_call(
        _memkv_kernel,
        grid=(depth, b),
        in_specs=[
            pl.BlockSpec((1, m, d), lambda l, i: (i, 0, 0)),
            pl.BlockSpec((1, d), lambda l, i: (0, 0)),
            pl.BlockSpec((1, d, n), lambda l, i: (l, 0, 0)),
        ],
        out_specs=pl.BlockSpec((1, 1, m, n), lambda l, i: (l, i, 0, 0)),
        out_shape=jax.ShapeDtypeStruct((depth, b, m, n), BF16),
        compiler_params=_params(2, vmem),
        name="memkv",
    )(mem, mem_g, wkv)


def _inproj_kernel(x_ref, g_ref, w_ref, wif_ref, bif_ref, cw_ref, cb_ref, wg_ref, bg_ref,
                   lam_ref, seg_ref, ya_ref, ifg_ref,
                   xn_scr, pc_scr, gelu_scr, xc_scr, xcb_scr, gt_scr, a_scr, u_scr,
                   carry_scr, hlast_scr, *, tm, rc, n_blk, n_steps):
    m = pl.program_id(1)
    seg_done = (m + N_SEG - 1) % N_SEG
    c = xn_scr.shape[1]
    halo = (CONV_W - 1) * V7X_SUBLANES
    gw = V7X_MXU_DIM
    chunks = range(0, tm, rc)
    runs = [(r0, slice(l0, l0 + V7X_LANES))
            for r0 in range(0, tm, TOK_BLOCK) for l0 in range(0, c, V7X_LANES)]

    def norm_and_gates():
        xn = _rms_norm(x_ref[0], g_ref[0]).astype(BF16)
        xn_scr[...] = xn
        ifg = lax.dot_general(wif_ref[0], xn, (((1,), (1,)), ((), ())),
                              preferred_element_type=F32)
        ifg_ref[0] = ifg + bif_ref[0]

    bm = tm // n_blk

    pieces = [(b, n0) for b in range(n_blk) for n0 in range(0, c, gw)]

    def project_piece(par, b, n0):
        pc_scr[par, halo + b * bm:halo + (b + 1) * bm, n0:n0 + gw] = jnp.dot(
            xn_scr[b * bm:(b + 1) * bm, :], w_ref[0, 0, :, n0:n0 + gw],
            preferred_element_type=F32)

    def project(par):
        for b, n0 in pieces:
            project_piece(par, b, n0)

    def interleave(par_next, items):
        n, k = len(items), len(pieces)
        for i, (b, n0) in enumerate(pieces):
            project_piece(par_next, b, n0)
            for item in items[i * n // k:(i + 1) * n // k]:
                item()

    def conv_run(par, ci, r0, lanes):
        base = halo + r0
        return _causal_conv_run(pc_scr[par, base - halo:base, lanes],
                                pc_scr[par, base:base + TOK_BLOCK, lanes],
                                cw_ref[0, ci][:, lanes], cb_ref[0, ci][:, lanes])

    def load_carry(par, ci):
        pc_scr[par, 0:halo, :] = carry_scr[ci]

    def save_carry(par, ci):
        carry_scr[ci] = pc_scr[par, tm:tm + halo, :]

    def gate_branch(par):
        def item(r0):
            gelu_scr[r0:r0 + rc, :] = jax.nn.gelu(pc_scr[par, halo + r0:halo + r0 + rc, :])
        return [functools.partial(item, r0) for r0 in chunks]

    def lru_conv(par):
        def conv_item(r0, lanes):
            xc = conv_run(par, 0, r0, lanes)
            xc_scr[r0:r0 + TOK_BLOCK, lanes] = xc
            xcb_scr[r0:r0 + TOK_BLOCK, lanes] = xc.astype(BF16)
        return ([functools.partial(load_carry, par, 0)]
                + [functools.partial(conv_item, r0, lanes) for r0, lanes in runs]
                + [functools.partial(save_carry, par, 0)])

    def lru_gates(groups):
        state = {}

        def gate_dot(g):
            sl = slice(g * gw, (g + 1) * gw)
            gt_scr[...] = jnp.dot(xcb_scr[:, sl], wg_ref[0, g], preferred_element_type=F32)
            state["c_row"] = -LRU_C * _softplus(-lam_ref[0][:, sl])

        def gate_item(g, r0):
            sl = slice(g * gw, (g + 1) * gw)
            r = jax.nn.sigmoid(gt_scr[r0:r0 + rc, :gw] + bg_ref[0, 0][:, sl])
            ig = jax.nn.sigmoid(gt_scr[r0:r0 + rc, gw:] + bg_ref[0, 1][:, sl])
            log_a = state["c_row"] * r
            a = jnp.exp(log_a)
            mult = jnp.sqrt(-jnp.tanh(log_a) * (a * a + 1.0))
            a_scr[r0:r0 + rc, sl] = a
            u_scr[r0:r0 + rc, sl] = mult * (ig * xc_scr[r0:r0 + rc, sl])

        items = []
        for g in groups:
            items.append(functools.partial(gate_dot, g))
            items += [functools.partial(gate_item, g, r0) for r0 in chunks]
        return items

    def lru_scan():
        state = {}

        def scan_item(r0, lanes):
            nsub = V7X_SUBLANES
            sub = lax.broadcasted_iota(jnp.int32, (nsub, V7X_LANES), 0)
            key = ("h", lanes.start)
            hprev = hlast_scr[:, lanes] if r0 == 0 else state[key]
            local, decay = [], []
            for r in range(TOK_SEG):
                rows = slice(r0 + r * nsub, r0 + (r + 1) * nsub)
                a = a_scr[rows, lanes]
                u = u_scr[rows, lanes]
                local.append(u if r == 0 else a * local[-1] + u)
                decay.append(a if r == 0 else a * decay[-1])
            ea, eu = decay[-1], local[-1]
            for k in (1, 2, 4):
                keep = sub >= k
                eu = jnp.where(keep, eu + ea * pltpu.roll(eu, k, 0), eu)
                ea = jnp.where(keep, ea * pltpu.roll(ea, k, 0), ea)
            ends = eu + ea * hprev
            h_in = pltpu.roll(jnp.where(sub == nsub - 1, hprev, ends), 1, 0)
            for r in range(TOK_SEG):
                rows = slice(r0 + r * nsub, r0 + (r + 1) * nsub)
                u_scr[rows, lanes] = local[r] + decay[r] * h_in
            hnew = jnp.broadcast_to(ends[nsub - 1:nsub, :], (nsub, V7X_LANES))
            state[key] = hnew
            if r0 + TOK_BLOCK == tm:
                hlast_scr[:, lanes] = hnew

        return [functools.partial(scan_item, r0, lanes) for r0, lanes in runs]

    def lru_out():
        def out_item(r0):
            ya = gelu_scr[r0:r0 + rc, :] * u_scr[r0:r0 + rc, :]
            ya_ref[0, r0:r0 + rc, :] = ya.astype(BF16)
        return [functools.partial(out_item, r0) for r0 in chunks]

    def conv_silu(ci, par):
        def item(r0, lanes):
            seg_ref[0, 0, r0:r0 + TOK_BLOCK, lanes] = jax.nn.silu(
                conv_run(par, ci, r0, lanes)).astype(BF16)
        return ([functools.partial(load_carry, par, ci)]
                + [functools.partial(item, r0, lanes) for r0, lanes in runs]
                + [functools.partial(save_carry, par, ci)])

    def plain(par):
        def item(r0):
            seg_ref[0, 0, r0:r0 + rc, :] = pc_scr[par, halo + r0:halo + r0 + rc, :].astype(BF16)
        return [functools.partial(item, r0) for r0 in chunks]

    n_groups = c // gw
    schedule = {
        SEG_GR: lambda par: gate_branch(par),
        SEG_XR: lambda par: lru_conv(par) + lru_gates(range(0, 1)),
        SEG_Q: lambda par: conv_silu(1, par),
        SEG_K: lambda par: conv_silu(2, par),
        SEG_V: lambda par: plain(par) + lru_gates(range(1, n_groups - 1)),
        SEG_O: lambda par: plain(par) + lru_gates(range(n_groups - 1, n_groups)) + lru_scan(),
        SEG_GA: lambda par: plain(par) + lru_out(),
        SEG_GB: lambda par: plain(par),
    }

    @pl.when(m == 0)
    def _():
        carry_scr[...] = jnp.zeros_like(carry_scr)
        hlast_scr[...] = jnp.zeros_like(hlast_scr)
        norm_and_gates()
        project(0)

    for j in range(N_SEG):
        @pl.when((m > 0) & (seg_done == j))
        def _(j=j):
            par = j % 2
            items = schedule[j](par)
            if j + 1 < N_SEG:
                interleave(1 - par, items)
            else:
                for item in items:
                    item()

                @pl.when(m < n_steps)
                def _():
                    norm_and_gates()
                    project(1 - par)


def _inproj(layer, h, norm_g, w_main, w_ifT, b_if, conv_w, conv_b, w_gate, b_gate, lam, *, tm=1024,
            rc=32, n_blk=4):
    b, s, d = h.shape
    n_groups = d // V7X_MXU_DIM
    halo = (CONV_W - 1) * V7X_SUBLANES
    assert tm % TOK_BLOCK == 0
    n_tiles = s // tm
    n_steps = n_tiles * N_SEG
    kern = functools.partial(_inproj_kernel, tm=tm, rc=rc, n_blk=n_blk, n_steps=n_steps)
    vmem = (2 * tm * d * 4 + 2 * d * d * 2 + 2 * tm * d * 2 + 2 * IFG_ROWS * tm * 4
            + tm * d * 2 + 2 * (tm + halo) * d * 4 + 4 * tm * d * 4 + tm * d * 2
            + 2 * n_groups * V7X_MXU_DIM * 2 * V7X_MXU_DIM * 2 + 3 * tm * d * 4)
    l = layer

    def proj_tile(m):
        return jnp.minimum(m // N_SEG, n_tiles - 1)

    def done_tile(m):
        return jnp.maximum(m - 1, 0) // N_SEG

    def done_slot(m):
        return jnp.maximum(jnp.maximum(m - 1, 0) % N_SEG - FIRST_STORED_SEG, 0)

    return pl.pallas_call(
        kern,
        grid=(b, n_steps + 1),
        in_specs=[
            pl.BlockSpec((1, tm, d), lambda bi, m: (bi, proj_tile(m), 0)),
            pl.BlockSpec((1, 1, d), lambda bi, m: (l, 0, 0)),
            pl.BlockSpec((1, 1, d, d), lambda bi, m: (l, m % N_SEG, 0, 0)),
            pl.BlockSpec((1, IFG_ROWS, d), lambda bi, m: (l, 0, 0)),
            pl.BlockSpec((1, IFG_ROWS, 1), lambda bi, m: (l, 0, 0)),
            pl.BlockSpec((1, 3, CONV_W, d), lambda bi, m: (l, 0, 0, 0)),
            pl.BlockSpec((1, 3, 1, d), lambda bi, m: (l, 0, 0, 0)),
            pl.BlockSpec((1, n_groups, V7X_MXU_DIM, 2 * V7X_MXU_DIM), lambda bi, m: (l, 0, 0, 0)),
            pl.BlockSpec((1, 2, 1, d), lambda bi, m: (l, 0, 0, 0)),
            pl.BlockSpec((1, 1, d), lambda bi, m: (l, 0, 0)),
        ],
        out_specs=[
            pl.BlockSpec((1, 1, tm, d), lambda bi, m: (done_slot(m), bi, done_tile(m), 0)),
            pl.BlockSpec((1, tm, d), lambda bi, m: (bi, done_tile(m), 0)),
            pl.BlockSpec((1, IFG_ROWS, tm), lambda bi, m: (bi, 0, proj_tile(m))),
        ],
        out_shape=[
            jax.ShapeDtypeStruct((N_SLOT, b, s, d), BF16),
            jax.ShapeDtypeStruct((b, s, d), BF16),
            jax.ShapeDtypeStruct((b, IFG_ROWS, s), F32),
        ],
        scratch_shapes=[
            pltpu.VMEM((tm, d), BF16),
            pltpu.VMEM((2, tm + halo, d), F32),
            pltpu.VMEM((tm, d), F32),
            pltpu.VMEM((tm, d), F32),
            pltpu.VMEM((tm, d), BF16),
            pltpu.VMEM((tm, 2 * V7X_MXU_DIM), F32),
            pltpu.VMEM((tm, d), F32),
            pltpu.VMEM((tm, d), F32),
            pltpu.VMEM((3, halo, d), F32),
            pltpu.VMEM((V7X_SUBLANES, d), F32),
        ],
        compiler_params=_params(2, vmem),
        name=f"inproj{layer}",
    )(h, norm_g, w_main, w_ifT, b_if, conv_w, conv_b, w_gate, b_gate, lam)


def _lane_scan(x, lane, op, fill):
    n = x.shape[-1]
    k = 1
    while k < n:
        x = op(x, jnp.where(lane >= k, pltpu.roll(x, k, 1), fill))
        k *= 2
    return x


_DONE = object()


def _mlstm_kernel(q_ref, k_ref, v_ref, o_ref, ifg_ref, ng_ref, y_ref, c_scr, n_scr, m_scr):
    @pl.when(pl.program_id(1) == 0)
    def _():
        c_scr[...] = jnp.zeros_like(c_scr)
        n_scr[...] = jnp.zeros_like(n_scr)
        m_scr[...] = jnp.zeros_like(m_scr)

    nb = y_ref.shape[0]
    gates = [_mlstm_gates(bb, ifg_ref, m_scr) for bb in range(nb)]
    heads = [_mlstm_head(bb, hd, gates[bb], q_ref, k_ref, v_ref, o_ref, ng_ref, y_ref, c_scr, n_scr)
             for hd in range(ML_HEADS) for bb in range(nb)]
    while heads:
        heads = [h for h in heads if next(h, _DONE) is not _DONE]


def _mlstm_gates(bb, ifg_ref, m_scr):
    L = ML_CHUNK
    rows = V7X_SUBLANES

    ifg = ifg_ref[bb]
    li = ifg[0:rows]
    lf = -_softplus(-ifg[rows:2 * rows])
    src = lax.broadcasted_iota(jnp.int32, (L, L), 0)
    dst = lax.broadcasted_iota(jnp.int32, (L, L), 1)
    not_after = (_time_of_row(src) <= _time_of_row(dst)).astype(F32)
    bcum = jnp.dot(lf, not_after, precision=lax.Precision.HIGHEST,
                   preferred_element_type=F32)
    m_st = m_scr[bb]
    g_tot = jnp.broadcast_to(bcum[:, L - 1:L], (rows, L))
    r1 = li - bcum
    u = g_tot - bcum + li
    m_next = jnp.maximum(g_tot + m_st, jnp.max(u, axis=-1, keepdims=True))
    decay = jnp.exp(g_tot + m_st - m_next)
    uexp = jnp.exp(u - m_next)
    m_scr[bb] = m_next

    stacked = jnp.concatenate(
        [bcum, m_st, uexp, jnp.zeros((L - 3 * rows, L), F32)], axis=0)
    cols = stacked.T
    return cols, r1, decay


def _mlstm_head(bb, hd, gates, q_ref, k_ref, v_ref, o_ref, ng_ref, y_ref, c_scr, n_scr):
    L = ML_CHUNK
    dh = c_scr.shape[2]
    rows = V7X_SUBLANES
    cols, r1, decay = gates
    tt = lax.broadcasted_iota(jnp.int32, (L, L), 0)
    ss = lax.broadcasted_iota(jnp.int32, (L, L), 1)
    causal = _time_of_row(ss) <= _time_of_row(tt)

    sl = slice(hd * dh, (hd + 1) * dh)
    qh = q_ref[0, bb, :, sl]
    kh = k_ref[0, bb, :, sl] * jnp.asarray(dh ** -0.5, BF16)
    vh = v_ref[0, bb, :, sl]
    b_col = cols[:, hd:hd + 1]
    a_inter = b_col + cols[:, rows + hd:rows + hd + 1]
    uec = cols[:, 2 * rows + hd:2 * rows + hd + 1]
    dec = decay[hd:hd + 1, 0:1]

    dmat = jnp.where(causal, b_col + r1[hd:hd + 1, :], -jnp.inf)
    m_t = jnp.maximum(a_inter, jnp.max(dmat, axis=-1, keepdims=True))
    qk = lax.dot_general(qh, kh, (((1,), (1,)), ((), ())), preferred_element_type=F32)
    c_old = c_scr[bb, hd]
    n_old = n_scr[bb, hd:hd + 1, :]
    qc = jnp.dot(qh, c_old.astype(BF16), preferred_element_type=F32)
    qn = jnp.sum(qh.astype(F32) * n_old, axis=-1, keepdims=True)
    yield
    sc = qk * jnp.exp(dmat - m_t)
    wic = jnp.exp(a_inter - m_t)
    enc = jnp.exp(-m_t)
    num = wic * qc + jnp.dot(sc.astype(BF16), vh, preferred_element_type=F32)
    den = wic * qn + jnp.sum(sc, axis=-1, keepdims=True)
    wk = kh.astype(F32) * uec
    c_scr[bb, hd] = dec * c_old + lax.dot_general(
        wk.astype(BF16), vh, (((0,), (0,)), ((), ())), preferred_element_type=F32)
    n_scr[bb, hd:hd + 1, :] = dec * n_old + jnp.sum(wk, axis=0, keepdims=True)
    yield
    hh = num * (1.0 / jnp.maximum(jnp.abs(den), enc))
    mu = jnp.mean(hh, axis=-1, keepdims=True)
    yield
    cen = hh - mu
    var = jnp.mean(cen * cen, axis=-1, keepdims=True)
    yield
    hn = (cen * lax.rsqrt(var + EPS)) * ng_ref[0][:, sl]
    gate = jax.nn.sigmoid(o_ref[0, bb, :, sl].astype(F32))
    y_ref[bb, :, sl] = (gate * hn).astype(BF16)


def _mlstm(layer, segs, ifg, ml_norm_g, *, nb=2):
    _, b, s, d = segs.shape
    L = ML_CHUNK
    dh = d // ML_HEADS
    l = layer

    def seg_spec(slot):
        return pl.BlockSpec((1, nb, L, d), lambda bi, ci: (slot, bi, ci, 0))

    vmem = nb * (2 * (4 * L * d * 2 + IFG_ROWS * L * 4 + L * d * 2) + ML_HEADS * dh * dh * 4
                 + 16 * L * d * 4)
    return pl.pallas_call(
        _mlstm_kernel,
        grid=(b // nb, s // L),
        in_specs=[
            seg_spec(SLOT_Q), seg_spec(SLOT_K), seg_spec(SLOT_V), seg_spec(SLOT_O),
            pl.BlockSpec((nb, IFG_ROWS, L), lambda bi, ci: (bi, 0, ci)),
            pl.BlockSpec((1, 1, d), lambda bi, ci: (l, 0, 0)),
        ],
        out_specs=pl.BlockSpec((nb, L, d), lambda bi, ci: (bi, ci, 0)),
        out_shape=jax.ShapeDtypeStruct((b, s, d), BF16),
        scratch_shapes=[
            pltpu.VMEM((nb, ML_HEADS, dh, dh), F32),
            pltpu.VMEM((nb, V7X_SUBLANES, dh), F32),
            pltpu.VMEM((nb, V7X_SUBLANES, L), F32),
        ],
        compiler_params=_params(2, vmem),
        name=f"mlstm{layer}",
    )(segs, segs, segs, segs, ifg, ml_norm_g)


def _merge_kernel(ya_ref, yb_ref, ga_ref, gb_ref, h_ref, wa_ref, wb_ref, wm_ref, o_ref):
    pa = jnp.dot(ya_ref[0], wa_ref[0], preferred_element_type=F32)
    pb = jnp.dot(yb_ref[0], wb_ref[0], preferred_element_type=F32)
    y = jax.nn.sigmoid(ga_ref[0, 0].astype(F32)) * pa + jax.nn.sigmoid(gb_ref[0, 0].astype(F32)) * pb
    o_ref[0] = h_ref[0] + jnp.dot(y.astype(BF16), wm_ref[0], preferred_element_type=F32)


def _merge(layer, ya, yb, segs, h, wa, wb, wm, *, tm=512):
    b, s, d = h.shape
    l = layer

    def seg_spec(slot):
        return pl.BlockSpec((1, 1, tm, d), lambda bi, i: (slot, bi, i, 0))

    w_spec = pl.BlockSpec((1, d, d), lambda bi, i: (l, 0, 0))
    row_spec = pl.BlockSpec((1, tm, d), lambda bi, i: (bi, i, 0))
    vmem = 2 * (4 * tm * d * 2 + 2 * tm * d * 4 + 3 * d * d * 2) + 4 * tm * d * 4
    return pl.pallas_call(
        _merge_kernel,
        grid=(b, s // tm),
        in_specs=[row_spec, row_spec, seg_spec(SLOT_GA), seg_spec(SLOT_GB), row_spec,
                  w_spec, w_spec, w_spec],
        out_specs=row_spec,
        out_shape=jax.ShapeDtypeStruct((b, s, d), F32),
        compiler_params=_params(2, vmem),
        name=f"merge{layer}",
    )(ya, yb, segs, segs, h, wa, wb, wm)


def _xattn_kernel(h_ref, g_ref, wq_ref, kv_ref, wo_ref, o_ref, att_scr):
    d = h_ref.shape[2]
    dh = d // XA_HEADS
    hres = h_ref[0]
    xn = _rms_norm(hres, g_ref[0]).astype(BF16)
    q = jnp.dot(xn, wq_ref[0], preferred_element_type=F32).astype(BF16)
    scale = dh ** -0.5
    for hd in range(XA_HEADS):
        sl = slice(hd * dh, (hd + 1) * dh)
        kh = kv_ref[0, 0, :, sl]
        vh = kv_ref[0, 0, :, d + hd * dh:d + (hd + 1) * dh]
        sc = lax.dot_general(q[:, sl], kh, (((1,), (1,)), ((), ())),
                             preferred_element_type=F32) * scale
        e = jnp.exp(sc - jnp.max(sc, axis=-1, keepdims=True))
        p = e * (1.0 / jnp.sum(e, axis=-1, keepdims=True))
        att_scr[:, sl] = jnp.dot(p.astype(BF16), vh, preferred_element_type=F32).astype(BF16)
    o_ref[0] = hres + jnp.dot(att_scr[...], wo_ref[0], preferred_element_type=F32)


def _xattn(layer, h, norm_g, wq, kv, wo, *, tm=512):
    b, s, d = h.shape
    m = kv.shape[2]
    l = layer
    row_spec = pl.BlockSpec((1, tm, d), lambda bi, i: (bi, i, 0))
    w_spec = pl.BlockSpec((1, d, d), lambda bi, i: (l, 0, 0))
    vmem = 2 * (2 * tm * d * 4 + 2 * d * d * 2 + m * 2 * d * 2) + tm * d * 2 + 6 * tm * d * 4
    return pl.pallas_call(
        _xattn_kernel,
        grid=(b, s // tm),
        in_specs=[
            row_spec,
            pl.BlockSpec((1, 1, d), lambda bi, i: (l, 0, 0)),
            w_spec,
            pl.BlockSpec((1, 1, m, 2 * d), lambda bi, i: (l, bi, 0, 0)),
            w_spec,
        ],
        out_specs=row_spec,
        out_shape=jax.ShapeDtypeStruct((b, s, d), F32),
        scratch_shapes=[pltpu.VMEM((tm, d), BF16)],
        compiler_params=_params(2, vmem),
        name=f"xattn{layer}",
    )(h, norm_g, wq, kv, wo)


def _ffn_kernel(x_ref, g_ref, wg_ref, wu_ref, cwg_ref, cwu_ref, cbg_ref, cbu_ref, wd_ref, fg_ref,
                o_ref, xn_scr, acc_scr, pc_scr, act_scr, carry_scr, *, tm, rc, n_j, n_blk,
                final_norm):
    i = pl.program_id(1)
    j = pl.program_id(2)
    halo = (FFN_CONV_W - 1) * V7X_SUBLANES
    tf = act_scr.shape[1]

    @pl.when(j == 0)
    def _():
        xn_scr[...] = _rms_norm(x_ref[0], g_ref[0]).astype(BF16)
        acc_scr[...] = jnp.zeros_like(acc_scr)

    @pl.when(i == 0)
    def _():
        carry_scr[j] = jnp.zeros(carry_scr.shape[1:], F32)

    for half in range(2):
        pc_scr[half, 0:halo, :] = carry_scr[j, half]

    def conv_run(half, cw_ref, cb_ref, r0, lanes):
        base = halo + r0
        return _causal_conv_run(pc_scr[half, base - halo:base, lanes],
                                pc_scr[half, base:base + TOK_BLOCK, lanes],
                                cw_ref[0][:, lanes], cb_ref[0][:, lanes])

    bm = tm // n_blk

    def up(b):
        for half, w_ref in enumerate((wg_ref, wu_ref)):
            pc_scr[half, halo + b * bm:halo + (b + 1) * bm, :] = jnp.dot(
                xn_scr[b * bm:(b + 1) * bm, :], w_ref[0, 0], preferred_element_type=F32)

    def epilogue(b):
        for r0 in range(b * bm, (b + 1) * bm, TOK_BLOCK):
            for l0 in range(0, tf, V7X_LANES):
                lanes = slice(l0, l0 + V7X_LANES)
                gpre = conv_run(0, cwg_ref, cbg_ref, r0, lanes)
                upre = conv_run(1, cwu_ref, cbu_ref, r0, lanes)
                act_scr[r0:r0 + TOK_BLOCK, lanes] = (jax.nn.gelu(gpre) * upre).astype(BF16)

    def down(b):
        rows = slice(b * bm, (b + 1) * bm)
        acc_scr[rows, :] += jnp.dot(act_scr[rows, :], wd_ref[0], preferred_element_type=F32)

    up(0)
    for b in range(n_blk):
        if b + 1 < n_blk:
            up(b + 1)
        epilogue(b)
        down(b)

    for half in range(2):
        carry_scr[j, half] = pc_scr[half, tm:tm + halo, :]

    @pl.when(j == n_j - 1)
    def _():
        oc = 8 * rc

        def out_body(r0):
            res = x_ref[0, pl.ds(r0, oc), :] + acc_scr[pl.ds(r0, oc), :]
            if final_norm:
                res = _rms_norm(res, fg_ref[...])
            o_ref[0, pl.ds(r0, oc), :] = res
        _row_chunks(tm, oc, out_body)


def _ffn(layer, h, norm_g, w_up, conv_w, conv_b, w_down, final_g, *, final_norm, tm=1024, rc=32,
         n_blk=4):
    b, s, d = h.shape
    tf = w_up.shape[3]
    n_j = w_up.shape[1] // 2
    halo = (FFN_CONV_W - 1) * V7X_SUBLANES
    assert (tm // n_blk) % TOK_BLOCK == 0
    l = layer
    kern = functools.partial(_ffn_kernel, tm=tm, rc=rc, n_j=n_j, n_blk=n_blk,
                             final_norm=final_norm)
    row_spec = pl.BlockSpec((1, tm, d), lambda bi, i, j: (bi, i, 0))
    vmem = (4 * tm * d * 4 + 2 * (2 * d * tf * 2 + tf * d * 2) + tm * d * 2 + tm * d * 4
            + 2 * (tm + halo) * tf * 4 + tm * tf * 2 + n_j * 2 * halo * tf * 4 + 2 * tm * tf * 4)
    return pl.pallas_call(
        kern,
        grid=(b, s // tm, n_j),
        in_specs=[
            row_spec,
            pl.BlockSpec((1, 1, d), lambda bi, i, j: (l, 0, 0)),
            pl.BlockSpec((1, 1, d, tf), lambda bi, i, j: (l, j, 0, 0)),
            pl.BlockSpec((1, 1, d, tf), lambda bi, i, j: (l, n_j + j, 0, 0)),
            pl.BlockSpec((1, FFN_CONV_W, tf), lambda bi, i, j: (l, 0, j)),
            pl.BlockSpec((1, FFN_CONV_W, tf), lambda bi, i, j: (l, 0, n_j + j)),
            pl.BlockSpec((1, 1, tf), lambda bi, i, j: (l, 0, j)),
            pl.BlockSpec((1, 1, tf), lambda bi, i, j: (l, 0, n_j + j)),
            pl.BlockSpec((1, tf, d), lambda bi, i, j: (l, j, 0)),
            pl.BlockSpec((1, d), lambda bi, i, j: (0, 0)),
        ],
        out_specs=row_spec,
        out_shape=jax.ShapeDtypeStruct((b, s, d), F32),
        scratch_shapes=[
            pltpu.VMEM((tm, d), BF16),
            pltpu.VMEM((tm, d), F32),
            pltpu.VMEM((2, tm + halo, tf), F32),
            pltpu.VMEM((tm, tf), BF16),
            pltpu.VMEM((n_j, 2, halo, tf), F32),
        ],
        compiler_params=_params(3, vmem),
        name=f"ffn{layer}",
    )(h, norm_g, w_up, w_up, conv_w, conv_w, conv_b, conv_b, w_down, final_g)


def _block_diag_groups(w):
    depth, n_blocks, bs, _ = w.shape
    per = V7X_MXU_DIM // bs
    wg = w.reshape(depth, n_blocks // per, per, bs, bs)
    eye = jnp.eye(per, dtype=w.dtype)
    out = wg[:, :, :, :, None, :] * eye[None, None, :, None, :, None]
    return out.reshape(depth, n_blocks // per, per * bs, per * bs)


def kernel(x, mem, norm_mix_g, w_in, rnn_conv_w, rnn_conv_b, lru_wa, lru_ba, lru_wx, lru_bx, lru_lambda, ml_conv_w, ml_conv_b, ml_if_b, ml_norm_g, w_branch_a, w_branch_b, w_mix_out, norm_xa_g, xa_wq, xa_wkv, xa_wo, norm_ffn_g, ffn_w_up, ffn_conv_w, ffn_conv_b, ffn_w_down, mem_norm_g, final_norm_g):
    depth, d, _ = w_in.shape
    d_ml = ml_norm_g.shape[1]
    assert d == d_ml == lru_lambda.shape[1], "kernels assume D_RNN == D_ML == D_MODEL"
    assert ML_HEADS <= V7X_SUBLANES and x.shape[1] % ML_CHUNK == 0

    o_xr, o_gr, o_q, o_k, o_v, o_o = (n * d for n in range(6))
    o_if = 6 * d
    o_ga = o_if + 2 * ML_HEADS
    o_gb = o_ga + d

    def cols(o):
        return w_in[:, :, o:o + d]

    w_main = jnp.stack(
        [cols(o_gr), cols(o_xr), cols(o_q), cols(o_k), cols(o_v), cols(o_o), cols(o_ga), cols(o_gb)],
        axis=1).astype(BF16)
    w_if = jnp.swapaxes(w_in[:, :, o_if:o_ga], 1, 2)
    pad = jnp.zeros((depth, V7X_SUBLANES - ML_HEADS, d), w_in.dtype)
    w_ifT = jnp.concatenate([w_if[:, :ML_HEADS], pad, w_if[:, ML_HEADS:], pad], axis=1).astype(BF16)
    bpad = jnp.zeros((depth, V7X_SUBLANES - ML_HEADS), ml_if_b.dtype)
    b_if = jnp.concatenate([ml_if_b[:, :ML_HEADS], bpad, ml_if_b[:, ML_HEADS:], bpad], axis=1)[..., None]

    conv_w = jnp.stack([rnn_conv_w, ml_conv_w[:, :, :d], ml_conv_w[:, :, d:]], axis=1)
    conv_b = jnp.stack([rnn_conv_b, ml_conv_b[:, :d], ml_conv_b[:, d:]], axis=1)[:, :, None, :]
    w_gate = jnp.concatenate([_block_diag_groups(lru_wa), _block_diag_groups(lru_wx)], axis=-1).astype(BF16)
    b_gate = jnp.stack([lru_ba, lru_bx], axis=1)[:, :, None, :]

    def row(g):
        return g[:, None, :]

    wa, wb, wm = (w.astype(BF16) for w in (w_branch_a, w_branch_b, w_mix_out))
    wq, wkv, wo = (w.astype(BF16) for w in (xa_wq, xa_wkv, xa_wo))
    w_down = ffn_w_down.astype(BF16)
    w_up = ffn_w_up.reshape(depth, d, -1, FFN_TILE)
    w_up = jnp.swapaxes(w_up, 1, 2).astype(BF16)
    ffn_cb = row(ffn_conv_b)

    kv = _memkv(mem, mem_norm_g[None, :], wkv)
    h = _to_block_order(x)
    for l in range(depth):
        segs, ya, ifg = _inproj(l, h, row(norm_mix_g), w_main, w_ifT, b_if, conv_w, conv_b, w_gate,
                                b_gate, row(lru_lambda))
        yb = _mlstm(l, segs, ifg, row(ml_norm_g))
        h = _merge(l, ya, yb, segs, h, wa, wb, wm)
        h = _xattn(l, h, row(norm_xa_g), wq, kv, wo)
        h = _ffn(l, h, row(norm_ffn_g), w_up, ffn_conv_w, ffn_cb, w_down, final_norm_g[None, :],
                 final_norm=(l == depth - 1))
    return _from_block_order(h)
```

```python
import functools

import jax
import jax.numpy as jnp
from jax import lax
from jax.experimental import pallas as pl
from jax.experimental.pallas import tpu as pltpu

F32 = jnp.float32
BF16 = jnp.bfloat16

EPS = 1e-6
LRU_C = 8.0
CONV_W = 4
FFN_CONV_W = 3
RNN_BLOCK = 64
ML_HEADS = 4
ML_CHUNK = 128
XA_HEADS = 4
FFN_TILE = 1024

V7X_SUBLANES = 8
V7X_LANES = 128
V7X_MXU_DIM = 256
V7X_VMEM_BYTES = 64 * 1024 * 1024
V7X_VMEM_REQUEST_CAP = 56 * 1024 * 1024

SEG_GR, SEG_XR, SEG_Q, SEG_K, SEG_V, SEG_O, SEG_GA, SEG_GB = range(8)
SLOT_Q, SLOT_K, SLOT_V, SLOT_O, SLOT_GA, SLOT_GB = range(6)
N_SEG = 8
N_SLOT = 6
FIRST_STORED_SEG = SEG_Q
IFG_ROWS = 16


def _vmem_limit(n_bytes):
    return int(min(V7X_VMEM_REQUEST_CAP, n_bytes + n_bytes // 2 + (4 << 20)))


def _params(n_grid, vmem_bytes):
    return pltpu.CompilerParams(
        dimension_semantics=("arbitrary",) * n_grid,
        vmem_limit_bytes=_vmem_limit(vmem_bytes),
    )


def _rms_norm(xf, g):
    ms = jnp.mean(xf * xf, axis=-1, keepdims=True)
    return (xf * lax.rsqrt(ms + EPS)) * g


def _gelu_tanh(x):
    c0 = (2.0 / jnp.pi) ** 0.5
    c1 = 0.044715 * c0
    half = 0.5 * x
    return half + half * jnp.tanh(x * (c0 + c1 * (x * x)))


def _softplus(z):
    return jnp.maximum(z, 0.0) + jnp.log1p(jnp.exp(-jnp.abs(z)))


def _row_chunks(n_rows, chunk, body):
    assert n_rows % chunk == 0

    def step(r, carry):
        body(pl.multiple_of(r * chunk, chunk))
        return carry

    lax.fori_loop(0, n_rows // chunk, step, 0)


TOK_BLOCK = ML_CHUNK
TOK_SEG = TOK_BLOCK // V7X_SUBLANES


def _to_block_order(t):
    b, s, d = t.shape
    t = t.reshape(b, s // TOK_BLOCK, V7X_SUBLANES, TOK_SEG, d)
    return jnp.swapaxes(t, 2, 3).reshape(b, s, d)


def _from_block_order(t):
    b, s, d = t.shape
    t = t.reshape(b, s // TOK_BLOCK, TOK_SEG, V7X_SUBLANES, d)
    return jnp.swapaxes(t, 2, 3).reshape(b, s, d)


def _time_of_row(row):
    return (row % V7X_SUBLANES) * TOK_SEG + row // V7X_SUBLANES


def _row_of_time(tau):
    return (tau % TOK_SEG) * V7X_SUBLANES + tau // TOK_SEG


def _halo_vregs(prev_tail, cur_tail):
    sub = lax.broadcasted_iota(jnp.int32, (V7X_SUBLANES, cur_tail.shape[1]), 0)
    out = []
    for i in range(cur_tail.shape[0] // V7X_SUBLANES):
        rows = slice(i * V7X_SUBLANES, (i + 1) * V7X_SUBLANES)
        mixed = jnp.where(sub == V7X_SUBLANES - 1, prev_tail[rows], cur_tail[rows])
        out.append(pltpu.roll(mixed, 1, 0))
    return out


def _causal_conv_run(prev_tail, cur, cw, cb):
    taps = cw.shape[0]
    n = cur.shape[0]
    tail = (taps - 1) * V7X_SUBLANES
    ext = jnp.concatenate(_halo_vregs(prev_tail, cur[n - tail:]) + [cur], axis=0)
    out = cb
    for k in range(taps):
        out = out + ext[k * V7X_SUBLANES:k * V7X_SUBLANES + n] * cw[k:k + 1]
    return out


def _memkv_kernel(mem_ref, g_ref, w_ref, o_ref):
    memn = _rms_norm(mem_ref[0], g_ref[...]).astype(BF16)
    o_ref[0, 0] = jnp.dot(memn, w_ref[0], preferred_element_type=F32).astype(BF16)


def _memkv(mem, mem_g, wkv):
    depth, d, n = wkv.shape
    b, m, _ = mem.shape
    vmem = 2 * (m * d * 4 + d * n * 2 + m * n * 2) + m * n * 4
    return pl.pallas_call(
        _memkv_kernel,
        grid=(depth, b),
        in_specs=[
            pl.BlockSpec((1, m, d), lambda l, i: (i, 0, 0)),
            pl.BlockSpec((1, d), lambda l, i: (0, 0)),
            pl.BlockSpec((1, d, n), lambda l, i: (l, 0, 0)),
        ],
        out_specs=pl.BlockSpec((1, 1, m, n), lambda l, i: (l, i, 0, 0)),
        out_shape=jax.ShapeDtypeStruct((depth, b, m, n), BF16),
        compiler_params=_params(2, vmem),
        name="memkv",
    )(mem, mem_g, wkv)


def _inproj_kernel(x_ref, g_ref, w_ref, wif_ref, bif_ref, cw_ref, cb_ref, wg_ref, bg_ref,
                   lam_ref, seg_ref, ya_ref, ifg_ref,
                   xn_scr, pc_scr, gelu_scr, xc_scr, xcb_scr, gt_scr, a_scr, u_scr,
                   carry_scr, hlast_scr, *, tm, rc, n_blk, n_steps):
    m = pl.program_id(1)
    seg_done = (m + N_SEG - 1) % N_SEG
    c = xn_scr.shape[1]
    halo = (CONV_W - 1) * V7X_SUBLANES
    gw = V7X_MXU_DIM
    chunks = range(0, tm, rc)
    runs = [(r0, slice(l0, l0 + V7X_LANES))
            for r0 in range(0, tm, TOK_BLOCK) for l0 in range(0, c, V7X_LANES)]

    def norm_and_gates():
        xn = _rms_norm(x_ref[0], g_ref[0]).astype(BF16)
        xn_scr[...] = xn
        ifg = lax.dot_general(wif_ref[0], xn, (((1,), (1,)), ((), ())),
                              preferred_element_type=F32)
        ifg_ref[0] = ifg + bif_ref[0]

    bm = tm // n_blk

    pieces = [(b, n0) for b in range(n_blk) for n0 in range(0, c, gw)]

    def project_piece(par, b, n0):
        pc_scr[par, halo + b * bm:halo + (b + 1) * bm, n0:n0 + gw] = jnp.dot(
            xn_scr[b * bm:(b + 1) * bm, :], w_ref[0, 0, :, n0:n0 + gw],
            preferred_element_type=F32)

    def project(par):
        for b, n0 in pieces:
            project_piece(par, b, n0)

    def interleave(par_next, items):
        n, k = len(items), len(pieces)
        for i, (b, n0) in enumerate(pieces):
            project_piece(par_next, b, n0)
            for item in items[i * n // k:(i + 1) * n // k]:
                item()

    def conv_run(par, ci, r0, lanes):
        base = halo + r0
        return _causal_conv_run(pc_scr[par, base - halo:base, lanes],
                                pc_scr[par, base:base + TOK_BLOCK, lanes],
                                cw_ref[0, ci][:, lanes], cb_ref[0, ci][:, lanes])

    def load_carry(par, ci):
        pc_scr[par, 0:halo, :] = carry_scr[ci]

    def save_carry(par, ci):
        carry_scr[ci] = pc_scr[par, tm:tm + halo, :]

    def gate_branch(par):
        def item(r0):
            gelu_scr[r0:r0 + rc, :] = _gelu_tanh(pc_scr[par, halo + r0:halo + r0 + rc, :])
        return [functools.partial(item, r0) for r0 in chunks]

    def lru_conv(par):
        def conv_item(r0, lanes):
            xc = conv_run(par, 0, r0, lanes)
            xc_scr[r0:r0 + TOK_BLOCK, lanes] = xc
            xcb_scr[r0:r0 + TOK_BLOCK, lanes] = xc.astype(BF16)
        return ([functools.partial(load_carry, par, 0)]
                + [functools.partial(conv_item, r0, lanes) for r0, lanes in runs]
                + [functools.partial(save_carry, par, 0)])

    def lru_gates(groups):
        state = {}

        def gate_dot(g):
            sl = slice(g * gw, (g + 1) * gw)
            gt_scr[...] = jnp.dot(xcb_scr[:, sl], wg_ref[0, g], preferred_element_type=F32)
            state["c_row"] = -LRU_C * _softplus(-lam_ref[0][:, sl])

        def gate_item(g, r0):
            sl = slice(g * gw, (g + 1) * gw)
            r = jax.nn.sigmoid(gt_scr[r0:r0 + rc, :gw] + bg_ref[0, 0][:, sl])
            ig = jax.nn.sigmoid(gt_scr[r0:r0 + rc, gw:] + bg_ref[0, 1][:, sl])
            log_a = state["c_row"] * r
            a = jnp.exp(log_a)
            mult = jnp.sqrt(-jnp.tanh(log_a) * (a * a + 1.0))
            a_scr[r0:r0 + rc, sl] = a
            u_scr[r0:r0 + rc, sl] = mult * (ig * xc_scr[r0:r0 + rc, sl])

        items = []
        for g in groups:
            items.append(functools.partial(gate_dot, g))
            items += [functools.partial(gate_item, g, r0) for r0 in chunks]
        return items

    def lru_scan():
        state = {}

        def scan_item(r0, lanes):
            nsub = V7X_SUBLANES
            sub = lax.broadcasted_iota(jnp.int32, (nsub, V7X_LANES), 0)
            key = ("h", lanes.start)
            hprev = hlast_scr[:, lanes] if r0 == 0 else state[key]
            local, decay = [], []
            for r in range(TOK_SEG):
                rows = slice(r0 + r * nsub, r0 + (r + 1) * nsub)
                a = a_scr[rows, lanes]
                u = u_scr[rows, lanes]
                local.append(u if r == 0 else a * local[-1] + u)
                decay.append(a if r == 0 else a * decay[-1])
            ea, eu = decay[-1], local[-1]
            for k in (1, 2, 4):
                keep = sub >= k
                eu = jnp.where(keep, eu + ea * pltpu.roll(eu, k, 0), eu)
                ea = jnp.where(keep, ea * pltpu.roll(ea, k, 0), ea)
            ends = eu + ea * hprev
            h_in = pltpu.roll(jnp.where(sub == nsub - 1, hprev, ends), 1, 0)
            for r in range(TOK_SEG):
                rows = slice(r0 + r * nsub, r0 + (r + 1) * nsub)
                u_scr[rows, lanes] = local[r] + decay[r] * h_in
            hnew = jnp.broadcast_to(ends[nsub - 1:nsub, :], (nsub, V7X_LANES))
            state[key] = hnew
            if r0 + TOK_BLOCK == tm:
                hlast_scr[:, lanes] = hnew

        return [functools.partial(scan_item, r0, lanes) for r0, lanes in runs]

    def lru_out():
        def out_item(r0):
            ya = gelu_scr[r0:r0 + rc, :] * u_scr[r0:r0 + rc, :]
            ya_ref[0, r0:r0 + rc, :] = ya.astype(BF16)
        return [functools.partial(out_item, r0) for r0 in chunks]

    def conv_silu(ci, par):
        def item(r0, lanes):
            half = 0.5 * conv_run(par, ci, r0, lanes)
            seg_ref[0, 0, r0:r0 + TOK_BLOCK, lanes] = (half + half * jnp.tanh(half)).astype(BF16)
        return ([functools.partial(load_carry, par, ci)]
                + [functools.partial(item, r0, lanes) for r0, lanes in runs]
                + [functools.partial(save_carry, par, ci)])

    def plain(par):
        def item(r0):
            seg_ref[0, 0, r0:r0 + rc, :] = pc_scr[par, halo + r0:halo + r0 + rc, :].astype(BF16)
        return [functools.partial(item, r0) for r0 in chunks]

    n_groups = c // gw
    schedule = {
        SEG_GR: lambda par: gate_branch(par),
        SEG_XR: lambda par: lru_conv(par) + lru_gates(range(0, 1)),
        SEG_Q: lambda par: conv_silu(1, par),
        SEG_K: lambda par: conv_silu(2, par),
        SEG_V: lambda par: plain(par) + lru_gates(range(1, n_groups - 1)),
        SEG_O: lambda par: plain(par) + lru_gates(range(n_groups - 1, n_groups)) + lru_scan(),
        SEG_GA: lambda par: plain(par) + lru_out(),
        SEG_GB: lambda par: plain(par),
    }

    @pl.when(m == 0)
    def _():
        carry_scr[...] = jnp.zeros_like(carry_scr)
        hlast_scr[...] = jnp.zeros_like(hlast_scr)
        norm_and_gates()
        project(0)

    for j in range(N_SEG):
        @pl.when((m > 0) & (seg_done == j))
        def _(j=j):
            par = j % 2
            items = schedule[j](par)
            if j + 1 < N_SEG:
                interleave(1 - par, items)
            else:
                for item in items:
                    item()

                @pl.when(m < n_steps)
                def _():
                    norm_and_gates()
                    project(1 - par)


def _inproj(layer, h, norm_g, w_main, w_ifT, b_if, conv_w, conv_b, w_gate, b_gate, lam, *, tm=1024,
            rc=32, n_blk=4):
    b, s, d = h.shape
    n_groups = d // V7X_MXU_DIM
    halo = (CONV_W - 1) * V7X_SUBLANES
    assert tm % TOK_BLOCK == 0
    n_tiles = s // tm
    n_steps = n_tiles * N_SEG
    kern = functools.partial(_inproj_kernel, tm=tm, rc=rc, n_blk=n_blk, n_steps=n_steps)
    vmem = (2 * tm * d * 4 + 2 * d * d * 2 + 2 * tm * d * 2 + 2 * IFG_ROWS * tm * 4
            + tm * d * 2 + 2 * (tm + halo) * d * 4 + 4 * tm * d * 4 + tm * d * 2
            + 2 * n_groups * V7X_MXU_DIM * 2 * V7X_MXU_DIM * 2 + 3 * tm * d * 4)
    l = layer

    def proj_tile(m):
        return jnp.minimum(m // N_SEG, n_tiles - 1)

    def done_tile(m):
        return jnp.maximum(m - 1, 0) // N_SEG

    def done_slot(m):
        return jnp.maximum(jnp.maximum(m - 1, 0) % N_SEG - FIRST_STORED_SEG, 0)

    return pl.pallas_call(
        kern,
        grid=(b, n_steps + 1),
        in_specs=[
            pl.BlockSpec((1, tm, d), lambda bi, m: (bi, proj_tile(m), 0)),
            pl.BlockSpec((1, 1, d), lambda bi, m: (l, 0, 0)),
            pl.BlockSpec((1, 1, d, d), lambda bi, m: (l, m % N_SEG, 0, 0)),
            pl.BlockSpec((1, IFG_ROWS, d), lambda bi, m: (l, 0, 0)),
            pl.BlockSpec((1, IFG_ROWS, 1), lambda bi, m: (l, 0, 0)),
            pl.BlockSpec((1, 3, CONV_W, d), lambda bi, m: (l, 0, 0, 0)),
            pl.BlockSpec((1, 3, 1, d), lambda bi, m: (l, 0, 0, 0)),
            pl.BlockSpec((1, n_groups, V7X_MXU_DIM, 2 * V7X_MXU_DIM), lambda bi, m: (l, 0, 0, 0)),
            pl.BlockSpec((1, 2, 1, d), lambda bi, m: (l, 0, 0, 0)),
            pl.BlockSpec((1, 1, d), lambda bi, m: (l, 0, 0)),
        ],
        out_specs=[
            pl.BlockSpec((1, 1, tm, d), lambda bi, m: (done_slot(m), bi, done_tile(m), 0)),
            pl.BlockSpec((1, tm, d), lambda bi, m: (bi, done_tile(m), 0)),
            pl.BlockSpec((1, IFG_ROWS, tm), lambda bi, m: (bi, 0, proj_tile(m))),
        ],
        out_shape=[
            jax.ShapeDtypeStruct((N_SLOT, b, s, d), BF16),
            jax.ShapeDtypeStruct((b, s, d), BF16),
            jax.ShapeDtypeStruct((b, IFG_ROWS, s), F32),
        ],
        scratch_shapes=[
            pltpu.VMEM((tm, d), BF16),
            pltpu.VMEM((2, tm + halo, d), F32),
            pltpu.VMEM((tm, d), F32),
            pltpu.VMEM((tm, d), F32),
            pltpu.VMEM((tm, d), BF16),
            pltpu.VMEM((tm, 2 * V7X_MXU_DIM), F32),
            pltpu.VMEM((tm, d), F32),
            pltpu.VMEM((tm, d), F32),
            pltpu.VMEM((3, halo, d), F32),
            pltpu.VMEM((V7X_SUBLANES, d), F32),
        ],
        compiler_params=_params(2, vmem),
        name=f"inproj{layer}",
    )(h, norm_g, w_main, w_ifT, b_if, conv_w, conv_b, w_gate, b_gate, lam)


def _lane_scan(x, lane, op, fill):
    n = x.shape[-1]
    k = 1
    while k < n:
        x = op(x, jnp.where(lane >= k, pltpu.roll(x, k, 1), fill))
        k *= 2
    return x


_DONE = object()


def _mlstm_kernel(q_ref, k_ref, v_ref, ifg_ref, y_ref, c_scr, m_scr):
    @pl.when(pl.program_id(1) == 0)
    def _():
        c_scr[...] = jnp.zeros_like(c_scr)
        m_scr[...] = jnp.zeros_like(m_scr)

    nb = y_ref.shape[0]
    gates = [_mlstm_gates(bb, ifg_ref, m_scr) for bb in range(nb)]
    heads = [_mlstm_head(bb, hd, gates[bb], q_ref, k_ref, v_ref, y_ref, c_scr)
             for hd in range(ML_HEADS) for bb in range(nb)]
    while heads:
        heads = [h for h in heads if next(h, _DONE) is not _DONE]


def _mlstm_gates(bb, ifg_ref, m_scr):
    L = ML_CHUNK
    rows = V7X_SUBLANES

    ifg = ifg_ref[bb]
    li = ifg[0:rows]
    lf = -_softplus(-ifg[rows:2 * rows])
    src = lax.broadcasted_iota(jnp.int32, (L, L), 0)
    dst = lax.broadcasted_iota(jnp.int32, (L, L), 1)
    not_after = (_time_of_row(src) <= _time_of_row(dst)).astype(F32)
    bcum = jnp.dot(lf, not_after, precision=lax.Precision.HIGHEST,
                   preferred_element_type=F32)
    m_st = m_scr[bb]
    g_tot = jnp.broadcast_to(bcum[:, L - 1:L], (rows, L))
    r1 = li - bcum
    u = g_tot - bcum + li
    m_next = jnp.maximum(g_tot + m_st, jnp.max(u, axis=-1, keepdims=True))
    decay = jnp.exp(g_tot + m_st - m_next)
    uexp = jnp.exp(u - m_next)
    m_scr[bb] = m_next

    stacked = jnp.concatenate(
        [bcum, m_st, jnp.zeros((L - 2 * rows, L), F32)], axis=0)
    cols = stacked.T
    return cols, r1, decay, uexp


def _mlstm_head(bb, hd, gates, q_ref, k_ref, v_ref, y_ref, c_scr):
    L = ML_CHUNK
    dh = c_scr.shape[2]
    rows = V7X_SUBLANES
    cols, r1, decay, uexp = gates
    tt = lax.broadcasted_iota(jnp.int32, (L, L), 0)
    ss = lax.broadcasted_iota(jnp.int32, (L, L), 1)
    causal = _time_of_row(ss) <= _time_of_row(tt)

    sl = slice(hd * dh, (hd + 1) * dh)
    qh = q_ref[0, bb, :, sl]
    kh = k_ref[0, bb, :, sl] * jnp.asarray(dh ** -0.5, BF16)
    vh = jnp.concatenate([v_ref[0, bb, :, sl], jnp.ones((L, V7X_LANES), BF16)], axis=1)
    b_col = cols[:, hd:hd + 1]
    a_inter = b_col + cols[:, rows + hd:rows + hd + 1]
    dec = decay[hd:hd + 1, 0:1]
    r1_row = r1[hd:hd + 1, :]

    r1_max = jnp.max(jnp.where(causal, r1_row, -jnp.inf), axis=-1, keepdims=True)
    m_t = jnp.maximum(a_inter, b_col + r1_max)
    qk = lax.dot_general(qh, kh, (((1,), (1,)), ((), ())), preferred_element_type=F32)
    c_old = c_scr[bb, hd]
    qc = jnp.dot(qh, c_old.astype(BF16), preferred_element_type=F32)
    yield
    sc = qk * jnp.exp(jnp.where(causal, (b_col - m_t) + r1_row, -jnp.inf))
    wic = jnp.exp(a_inter - m_t)
    enc = jnp.exp(-m_t)
    both = wic * qc + jnp.dot(sc.astype(BF16), vh, preferred_element_type=F32)
    wk_t = (kh.T.astype(F32) * uexp[hd:hd + 1, :]).astype(BF16)
    c_scr[bb, hd] = dec * c_old + jnp.dot(wk_t, vh, preferred_element_type=F32)
    yield
    num = both[:, :dh]
    den = both[:, dh:dh + 1]
    hh = num * (1.0 / jnp.maximum(jnp.abs(den), enc))
    y_ref[bb, :, sl] = hh.astype(BF16)


def _mlstm(layer, segs, ifg, *, nb=4):
    _, b, s, d = segs.shape
    L = ML_CHUNK
    dh = d // ML_HEADS

    def seg_spec(slot):
        return pl.BlockSpec((1, nb, L, d), lambda bi, ci: (slot, bi, ci, 0))

    vmem = nb * (2 * (3 * L * d * 2 + IFG_ROWS * L * 4 + L * d * 2)
                 + ML_HEADS * dh * (dh + V7X_LANES) * 4 + 16 * L * d * 4)
    return pl.pallas_call(
        _mlstm_kernel,
        grid=(b // nb, s // L),
        in_specs=[
            seg_spec(SLOT_Q), seg_spec(SLOT_K), seg_spec(SLOT_V),
            pl.BlockSpec((nb, IFG_ROWS, L), lambda bi, ci: (bi, 0, ci)),
        ],
        out_specs=pl.BlockSpec((nb, L, d), lambda bi, ci: (bi, ci, 0)),
        out_shape=jax.ShapeDtypeStruct((b, s, d), BF16),
        scratch_shapes=[
            pltpu.VMEM((nb, ML_HEADS, dh, dh + V7X_LANES), F32),
            pltpu.VMEM((nb, V7X_SUBLANES, L), F32),
        ],
        compiler_params=_params(2, vmem),
        name=f"mlstm{layer}",
    )(segs, segs, segs, ifg)


def _merge_kernel(ya_ref, hm_ref, og_ref, ga_ref, gb_ref, h_ref, ng_ref, wa_ref, wb_ref, wm_ref,
                  o_ref, yb_scr, *, n_blk):
    tm, d = yb_scr.shape
    dh = d // ML_HEADS
    bm = tm // n_blk
    for blk in range(n_blk):
        rows = slice(blk * bm, (blk + 1) * bm)
        for r0 in range(blk * bm, (blk + 1) * bm, TOK_BLOCK):
            for hd in range(ML_HEADS):
                sl = slice(hd * dh, (hd + 1) * dh)
                hh = hm_ref[0, r0:r0 + TOK_BLOCK, sl].astype(F32)
                mu = jnp.mean(hh, axis=-1, keepdims=True)
                cen = hh - mu
                var = jnp.mean(cen * cen, axis=-1, keepdims=True)
                hn = (cen * lax.rsqrt(var + EPS)) * ng_ref[0][:, sl]
                gate = jax.nn.sigmoid(og_ref[0, 0, r0:r0 + TOK_BLOCK, sl].astype(F32))
                yb_scr[r0:r0 + TOK_BLOCK, sl] = (gate * hn).astype(BF16)
        pa = jnp.dot(ya_ref[0, rows, :], wa_ref[0], preferred_element_type=F32)
        pb = jnp.dot(yb_scr[rows, :], wb_ref[0], preferred_element_type=F32)
        y = (jax.nn.sigmoid(ga_ref[0, 0, rows, :].astype(F32)) * pa
             + jax.nn.sigmoid(gb_ref[0, 0, rows, :].astype(F32)) * pb)
        o_ref[0, rows, :] = h_ref[0, rows, :] + jnp.dot(y.astype(BF16), wm_ref[0],
                                                        preferred_element_type=F32)


def _merge(layer, ya, hm, segs, h, ml_norm_g, wa, wb, wm, *, tm=512, n_blk=2):
    b, s, d = h.shape
    l = layer

    def seg_spec(slot):
        return pl.BlockSpec((1, 1, tm, d), lambda bi, i: (slot, bi, i, 0))

    w_spec = pl.BlockSpec((1, d, d), lambda bi, i: (l, 0, 0))
    row_spec = pl.BlockSpec((1, tm, d), lambda bi, i: (bi, i, 0))
    vmem = 2 * (5 * tm * d * 2 + 2 * tm * d * 4 + 3 * d * d * 2) + tm * d * 2 + 4 * tm * d * 4
    return pl.pallas_call(
        functools.partial(_merge_kernel, n_blk=n_blk),
        grid=(b, s // tm),
        in_specs=[row_spec, row_spec, seg_spec(SLOT_O), seg_spec(SLOT_GA), seg_spec(SLOT_GB),
                  row_spec, pl.BlockSpec((1, 1, d), lambda bi, i: (l, 0, 0)),
                  w_spec, w_spec, w_spec],
        out_specs=row_spec,
        out_shape=jax.ShapeDtypeStruct((b, s, d), F32),
        scratch_shapes=[pltpu.VMEM((tm, d), BF16)],
        compiler_params=_params(2, vmem),
        name=f"merge{layer}",
    )(ya, hm, segs, segs, segs, h, ml_norm_g, wa, wb, wm)


def _xattn_kernel(h_ref, g_ref, wq_ref, kv_ref, wo_ref, o_ref, att_scr):
    d = h_ref.shape[2]
    dh = d // XA_HEADS
    hres = h_ref[0]
    xn = _rms_norm(hres, g_ref[0]).astype(BF16)
    q = jnp.dot(xn, wq_ref[0], preferred_element_type=F32).astype(BF16)
    scale = dh ** -0.5
    for hd in range(XA_HEADS):
        sl = slice(hd * dh, (hd + 1) * dh)
        kh = kv_ref[0, 0, :, sl]
        vh = kv_ref[0, 0, :, d + hd * dh:d + (hd + 1) * dh]
        sc = lax.dot_general(q[:, sl], kh, (((1,), (1,)), ((), ())),
                             preferred_element_type=F32) * scale
        e = jnp.exp(sc - jnp.max(sc, axis=-1, keepdims=True))
        p = e * (1.0 / jnp.sum(e, axis=-1, keepdims=True))
        att_scr[:, sl] = jnp.dot(p.astype(BF16), vh, preferred_element_type=F32).astype(BF16)
    o_ref[0] = hres + jnp.dot(att_scr[...], wo_ref[0], preferred_element_type=F32)


def _xattn(layer, h, norm_g, wq, kv, wo, *, tm=512):
    b, s, d = h.shape
    m = kv.shape[2]
    l = layer
    row_spec = pl.BlockSpec((1, tm, d), lambda bi, i: (bi, i, 0))
    w_spec = pl.BlockSpec((1, d, d), lambda bi, i: (l, 0, 0))
    vmem = 2 * (2 * tm * d * 4 + 2 * d * d * 2 + m * 2 * d * 2) + tm * d * 2 + 6 * tm * d * 4
    return pl.pallas_call(
        _xattn_kernel,
        grid=(b, s // tm),
        in_specs=[
            row_spec,
            pl.BlockSpec((1, 1, d), lambda bi, i: (l, 0, 0)),
            w_spec,
            pl.BlockSpec((1, 1, m, 2 * d), lambda bi, i: (l, bi, 0, 0)),
            w_spec,
        ],
        out_specs=row_spec,
        out_shape=jax.ShapeDtypeStruct((b, s, d), F32),
        scratch_shapes=[pltpu.VMEM((tm, d), BF16)],
        compiler_params=_params(2, vmem),
        name=f"xattn{layer}",
    )(h, norm_g, wq, kv, wo)


def _ffn_kernel(x_ref, g_ref, wg_ref, wu_ref, cwg_ref, cwu_ref, cbg_ref, cbu_ref, wd_ref, fg_ref,
                o_ref, xn_scr, acc_scr, pc_scr, act_scr, carry_scr, *, tm, rc, n_j, n_blk,
                final_norm):
    i = pl.program_id(1)
    j = pl.program_id(2)
    halo = (FFN_CONV_W - 1) * V7X_SUBLANES
    tf = act_scr.shape[1]

    @pl.when(j == 0)
    def _():
        xn_scr[...] = _rms_norm(x_ref[0], g_ref[0]).astype(BF16)
        acc_scr[...] = jnp.zeros_like(acc_scr)

    @pl.when(i == 0)
    def _():
        carry_scr[j] = jnp.zeros(carry_scr.shape[1:], F32)

    for half in range(2):
        pc_scr[half, 0:halo, :] = carry_scr[j, half]

    def conv_run(half, cw_ref, cb_ref, r0, lanes):
        base = halo + r0
        return _causal_conv_run(pc_scr[half, base - halo:base, lanes],
                                pc_scr[half, base:base + TOK_BLOCK, lanes],
                                cw_ref[0][:, lanes], cb_ref[0][:, lanes])

    bm = tm // n_blk

    def up(b):
        for half, w_ref in enumerate((wg_ref, wu_ref)):
            pc_scr[half, halo + b * bm:halo + (b + 1) * bm, :] = jnp.dot(
                xn_scr[b * bm:(b + 1) * bm, :], w_ref[0, 0], preferred_element_type=F32)

    def epilogue(b):
        for r0 in range(b * bm, (b + 1) * bm, TOK_BLOCK):
            for l0 in range(0, tf, V7X_LANES):
                lanes = slice(l0, l0 + V7X_LANES)
                gpre = conv_run(0, cwg_ref, cbg_ref, r0, lanes)
                upre = conv_run(1, cwu_ref, cbu_ref, r0, lanes)
                act_scr[r0:r0 + TOK_BLOCK, lanes] = (_gelu_tanh(gpre) * upre).astype(BF16)

    def down(b):
        rows = slice(b * bm, (b + 1) * bm)
        acc_scr[rows, :] += jnp.dot(act_scr[rows, :], wd_ref[0], preferred_element_type=F32)

    up(0)
    for b in range(n_blk):
        if b + 1 < n_blk:
            up(b + 1)
        epilogue(b)
        down(b)

    for half in range(2):
        carry_scr[j, half] = pc_scr[half, tm:tm + halo, :]

    @pl.when(j == n_j - 1)
    def _():
        oc = 8 * rc

        def out_body(r0):
            res = x_ref[0, pl.ds(r0, oc), :] + acc_scr[pl.ds(r0, oc), :]
            if final_norm:
                res = _rms_norm(res, fg_ref[...])
            o_ref[0, pl.ds(r0, oc), :] = res
        _row_chunks(tm, oc, out_body)


def _ffn(layer, h, norm_g, w_up, conv_w, conv_b, w_down, final_g, *, final_norm, tm=1024, rc=32,
         n_blk=4):
    b, s, d = h.shape
    tf = w_up.shape[3]
    n_j = w_up.shape[1] // 2
    halo = (FFN_CONV_W - 1) * V7X_SUBLANES
    assert (tm // n_blk) % TOK_BLOCK == 0
    l = layer
    kern = functools.partial(_ffn_kernel, tm=tm, rc=rc, n_j=n_j, n_blk=n_blk,
                             final_norm=final_norm)
    row_spec = pl.BlockSpec((1, tm, d), lambda bi, i, j: (bi, i, 0))
    vmem = (4 * tm * d * 4 + 2 * (2 * d * tf * 2 + tf * d * 2) + tm * d * 2 + tm * d * 4
            + 2 * (tm + halo) * tf * 4 + tm * tf * 2 + n_j * 2 * halo * tf * 4 + 2 * tm * tf * 4)
    return pl.pallas_call(
        kern,
        grid=(b, s // tm, n_j),
        in_specs=[
            row_spec,
            pl.BlockSpec((1, 1, d), lambda bi, i, j: (l, 0, 0)),
            pl.BlockSpec((1, 1, d, tf), lambda bi, i, j: (l, j, 0, 0)),
            pl.BlockSpec((1, 1, d, tf), lambda bi, i, j: (l, n_j + j, 0, 0)),
            pl.BlockSpec((1, FFN_CONV_W, tf), lambda bi, i, j: (l, 0, j)),
            pl.BlockSpec((1, FFN_CONV_W, tf), lambda bi, i, j: (l, 0, n_j + j)),
            pl.BlockSpec((1, 1, tf), lambda bi, i, j: (l, 0, j)),
            pl.BlockSpec((1, 1, tf), lambda bi, i, j: (l, 0, n_j + j)),
            pl.BlockSpec((1, tf, d), lambda bi, i, j: (l, j, 0)),
            pl.BlockSpec((1, d), lambda bi, i, j: (0, 0)),
        ],
        out_specs=row_spec,
        out_shape=jax.ShapeDtypeStruct((b, s, d), F32),
        scratch_shapes=[
            pltpu.VMEM((tm, d), BF16),
            pltpu.VMEM((tm, d), F32),
            pltpu.VMEM((2, tm + halo, tf), F32),
            pltpu.VMEM((tm, tf), BF16),
            pltpu.VMEM((n_j, 2, halo, tf), F32),
        ],
        compiler_params=_params(3, vmem),
        name=f"ffn{layer}",
    )(h, norm_g, w_up, w_up, conv_w, conv_w, conv_b, conv_b, w_down, final_g)


def _block_diag_groups(w):
    depth, n_blocks, bs, _ = w.shape
    per = V7X_MXU_DIM // bs
    wg = w.reshape(depth, n_blocks // per, per, bs, bs)
    eye = jnp.eye(per, dtype=w.dtype)
    out = wg[:, :, :, :, None, :] * eye[None, None, :, None, :, None]
    return out.reshape(depth, n_blocks // per, per * bs, per * bs)


def kernel(x, mem, norm_mix_g, w_in, rnn_conv_w, rnn_conv_b, lru_wa, lru_ba, lru_wx, lru_bx, lru_lambda, ml_conv_w, ml_conv_b, ml_if_b, ml_norm_g, w_branch_a, w_branch_b, w_mix_out, norm_xa_g, xa_wq, xa_wkv, xa_wo, norm_ffn_g, ffn_w_up, ffn_conv_w, ffn_conv_b, ffn_w_down, mem_norm_g, final_norm_g):
    depth, d, _ = w_in.shape
    d_ml = ml_norm_g.shape[1]
    assert d == d_ml == lru_lambda.shape[1], "kernels assume D_RNN == D_ML == D_MODEL"
    assert ML_HEADS <= V7X_SUBLANES and x.shape[1] % ML_CHUNK == 0

    o_xr, o_gr, o_q, o_k, o_v, o_o = (n * d for n in range(6))
    o_if = 6 * d
    o_ga = o_if + 2 * ML_HEADS
    o_gb = o_ga + d

    def cols(o):
        return w_in[:, :, o:o + d]

    w_main = jnp.stack(
        [cols(o_gr), cols(o_xr), cols(o_q), cols(o_k), cols(o_v), cols(o_o), cols(o_ga), cols(o_gb)],
        axis=1).astype(BF16)
    w_if = jnp.swapaxes(w_in[:, :, o_if:o_ga], 1, 2)
    pad = jnp.zeros((depth, V7X_SUBLANES - ML_HEADS, d), w_in.dtype)
    w_ifT = jnp.concatenate([w_if[:, :ML_HEADS], pad, w_if[:, ML_HEADS:], pad], axis=1).astype(BF16)
    bpad = jnp.zeros((depth, V7X_SUBLANES - ML_HEADS), ml_if_b.dtype)
    b_if = jnp.concatenate([ml_if_b[:, :ML_HEADS], bpad, ml_if_b[:, ML_HEADS:], bpad], axis=1)[..., None]

    conv_w = jnp.stack([rnn_conv_w, ml_conv_w[:, :, :d], ml_conv_w[:, :, d:]], axis=1)
    conv_b = jnp.stack([rnn_conv_b, ml_conv_b[:, :d], ml_conv_b[:, d:]], axis=1)[:, :, None, :]
    w_gate = jnp.concatenate([_block_diag_groups(lru_wa), _block_diag_groups(lru_wx)], axis=-1).astype(BF16)
    b_gate = jnp.stack([lru_ba, lru_bx], axis=1)[:, :, None, :]

    def row(g):
        return g[:, None, :]

    wa, wb, wm = (w.astype(BF16) for w in (w_branch_a, w_branch_b, w_mix_out))
    wq, wkv, wo = (w.astype(BF16) for w in (xa_wq, xa_wkv, xa_wo))
    w_down = ffn_w_down.astype(BF16)
    w_up = ffn_w_up.reshape(depth, d, -1, FFN_TILE)
    w_up = jnp.swapaxes(w_up, 1, 2).astype(BF16)
    ffn_cb = row(ffn_conv_b)

    kv = _memkv(mem, mem_norm_g[None, :], wkv)
    h = _to_block_order(x)
    for l in range(depth):
        segs, ya, ifg = _inproj(l, h, row(norm_mix_g), w_main, w_ifT, b_if, conv_w, conv_b, w_gate,
                                b_gate, row(lru_lambda))
        hm = _mlstm(l, segs, ifg)
        h = _merge(l, ya, hm, segs, h, row(ml_norm_g), wa, wb, wm)
        h = _xattn(l, h, row(norm_xa_g), wq, kv, wo)
        h = _ffn(l, h, row(norm_ffn_g), w_up, ffn_conv_w, ffn_cb, w_down, final_norm_g[None, :],
                 final_norm=(l == depth - 1))
    return _from_block_order(h)
```

```python
import functools

import jax
import jax.numpy as jnp
from jax import lax
from jax.experimental import pallas as pl
from jax.experimental.pallas import tpu as pltpu

F32 = jnp.float32
BF16 = jnp.bfloat16

EPS = 1e-6
LRU_C = 8.0
CONV_W = 4
FFN_CONV_W = 3
RNN_BLOCK = 64
ML_HEADS = 4
ML_CHUNK = 128
XA_HEADS = 4
FFN_TILE = 1024

V7X_SUBLANES = 8
V7X_LANES = 128
V7X_MXU_DIM = 256
V7X_VMEM_BYTES = 64 * 1024 * 1024
V7X_VMEM_REQUEST_CAP = 56 * 1024 * 1024

SEG_GR, SEG_XR, SEG_Q, SEG_K, SEG_V, SEG_O, SEG_GA, SEG_GB = range(8)
SLOT_Q, SLOT_K, SLOT_V, SLOT_O, SLOT_GA, SLOT_GB = range(6)
N_SEG = 8
N_SLOT = 6
FIRST_STORED_SEG = SEG_Q
IFG_ROWS = 16


def _vmem_limit(n_bytes):
    return int(min(V7X_VMEM_REQUEST_CAP, n_bytes + n_bytes // 2 + (4 << 20)))


def _params(n_grid, vmem_bytes):
    return pltpu.CompilerParams(
        dimension_semantics=("arbitrary",) * n_grid,
        vmem_limit_bytes=_vmem_limit(vmem_bytes),
    )


def _rms_norm(xf, g):
    ms = jnp.mean(xf * xf, axis=-1, keepdims=True)
    return (xf * lax.rsqrt(ms + EPS)) * g


def _gelu_tanh(x):
    c0 = (2.0 / jnp.pi) ** 0.5
    c1 = 0.044715 * c0
    half = 0.5 * x
    return half + half * jnp.tanh(x * (c0 + c1 * (x * x)))


def _softplus(z):
    return jnp.maximum(z, 0.0) + jnp.log1p(jnp.exp(-jnp.abs(z)))


def _row_chunks(n_rows, chunk, body):
    assert n_rows % chunk == 0

    def step(r, carry):
        body(pl.multiple_of(r * chunk, chunk))
        return carry

    lax.fori_loop(0, n_rows // chunk, step, 0)


TOK_BLOCK = ML_CHUNK
TOK_SEG = TOK_BLOCK // V7X_SUBLANES


def _to_block_order(t):
    b, s, d = t.shape
    t = t.reshape(b, s // TOK_BLOCK, V7X_SUBLANES, TOK_SEG, d)
    return jnp.swapaxes(t, 2, 3).reshape(b, s, d)


def _from_block_order(t):
    b, s, d = t.shape
    t = t.reshape(b, s // TOK_BLOCK, TOK_SEG, V7X_SUBLANES, d)
    return jnp.swapaxes(t, 2, 3).reshape(b, s, d)


def _time_of_row(row):
    return (row % V7X_SUBLANES) * TOK_SEG + row // V7X_SUBLANES


def _row_of_time(tau):
    return (tau % TOK_SEG) * V7X_SUBLANES + tau // TOK_SEG


def _halo_vregs(prev_tail, cur_tail):
    sub = lax.broadcasted_iota(jnp.int32, (V7X_SUBLANES, cur_tail.shape[1]), 0)
    out = []
    for i in range(cur_tail.shape[0] // V7X_SUBLANES):
        rows = slice(i * V7X_SUBLANES, (i + 1) * V7X_SUBLANES)
        mixed = jnp.where(sub == V7X_SUBLANES - 1, prev_tail[rows], cur_tail[rows])
        out.append(pltpu.roll(mixed, 1, 0))
    return out


def _replicate_rows(rows_1w):
    k, w = rows_1w.shape
    return jnp.broadcast_to(rows_1w[:, None, :], (k, V7X_SUBLANES, w))


def _causal_conv_run(prev_tail, cur, cwb):
    taps = cwb.shape[0] - 1
    nsub = V7X_SUBLANES
    n, w = cur.shape
    tail = (taps - 1) * nsub
    ext = jnp.concatenate(_halo_vregs(prev_tail, cur[n - tail:]) + [cur], axis=0)
    ext = ext.reshape(ext.shape[0] // nsub, nsub, w)
    out = cwb[taps][None]
    for k in range(taps):
        out = out + ext[k:k + n // nsub] * cwb[k][None]
    return out.reshape(n, w)


def _memkv_kernel(mem_ref, g_ref, w_ref, o_ref):
    memn = _rms_norm(mem_ref[0], g_ref[...]).astype(BF16)
    o_ref[0, 0] = jnp.dot(memn, w_ref[0], preferred_element_type=F32).astype(BF16)


def _memkv(mem, mem_g, wkv):
    depth, d, n = wkv.shape
    b, m, _ = mem.shape
    vmem = 2 * (m * d * 4 + d * n * 2 + m * n * 2) + m * n * 4
    return pl.pallas_call(
        _memkv_kernel,
        grid=(depth, b),
        in_specs=[
            pl.BlockSpec((1, m, d), lambda l, i: (i, 0, 0)),
            pl.BlockSpec((1, d), lambda l, i: (0, 0)),
            pl.BlockSpec((1, d, n), lambda l, i: (l, 0, 0)),
        ],
        out_specs=pl.BlockSpec((1, 1, m, n), lambda l, i: (l, i, 0, 0)),
        out_shape=jax.ShapeDtypeStruct((depth, b, m, n), BF16),
        compiler_params=_params(2, vmem),
        name="memkv",
    )(mem, mem_g, wkv)


def _inproj_kernel(x_ref, g_ref, w_ref, wif_ref, bif_ref, cw_ref, cb_ref, wg_ref, bg_ref,
                   lam_ref, seg_ref, ya_ref, ifg_ref,
                   xn_scr, pc_scr, gelu_scr, xc_scr, xcb_scr, gt_scr, a_scr, u_scr,
                   carry_scr, hlast_scr, cwb_scr, *, tm, rc, n_blk, n_steps):
    m = pl.program_id(1)
    seg_done = (m + N_SEG - 1) % N_SEG
    c = xn_scr.shape[1]
    halo = (CONV_W - 1) * V7X_SUBLANES
    gw = V7X_MXU_DIM
    chunks = range(0, tm, rc)
    runs = [(r0, slice(l0, l0 + V7X_LANES))
            for r0 in range(0, tm, TOK_BLOCK) for l0 in range(0, c, V7X_LANES)]

    def norm_and_gates():
        xn = _rms_norm(x_ref[0], g_ref[0]).astype(BF16)
        xn_scr[...] = xn
        ifg = lax.dot_general(wif_ref[0], xn, (((1,), (1,)), ((), ())),
                              preferred_element_type=F32)
        ifg_ref[0] = ifg + bif_ref[0]

    bm = tm // n_blk

    pieces = [(b, n0) for b in range(n_blk) for n0 in range(0, c, gw)]

    def project_piece(par, b, n0):
        pc_scr[par, halo + b * bm:halo + (b + 1) * bm, n0:n0 + gw] = jnp.dot(
            xn_scr[b * bm:(b + 1) * bm, :], w_ref[0, 0, :, n0:n0 + gw],
            preferred_element_type=F32)

    def project(par):
        for b, n0 in pieces:
            project_piece(par, b, n0)

    def interleave(par_next, items):
        n, k = len(items), len(pieces)
        for i, (b, n0) in enumerate(pieces):
            project_piece(par_next, b, n0)
            for item in items[i * n // k:(i + 1) * n // k]:
                item()

    def conv_run(par, ci, r0, lanes):
        base = halo + r0
        return _causal_conv_run(pc_scr[par, base - halo:base, lanes],
                                pc_scr[par, base:base + TOK_BLOCK, lanes],
                                cwb_scr[ci, :, :, lanes])

    def load_carry(par, ci):
        pc_scr[par, 0:halo, :] = carry_scr[ci]

    def save_carry(par, ci):
        carry_scr[ci] = pc_scr[par, tm:tm + halo, :]

    def gate_branch(par):
        def item(r0):
            gelu_scr[r0:r0 + rc, :] = _gelu_tanh(pc_scr[par, halo + r0:halo + r0 + rc, :])
        return [functools.partial(item, r0) for r0 in chunks]

    def lru_conv(par):
        def conv_item(r0, lanes):
            xc = conv_run(par, 0, r0, lanes)
            xc_scr[r0:r0 + TOK_BLOCK, lanes] = xc
            xcb_scr[r0:r0 + TOK_BLOCK, lanes] = xc.astype(BF16)
        return ([functools.partial(load_carry, par, 0)]
                + [functools.partial(conv_item, r0, lanes) for r0, lanes in runs]
                + [functools.partial(save_carry, par, 0)])

    def lru_gates(groups):
        state = {}

        def gate_dot(g):
            sl = slice(g * gw, (g + 1) * gw)
            gt_scr[...] = jnp.dot(xcb_scr[:, sl], wg_ref[0, g], preferred_element_type=F32)
            state["c_row"] = -LRU_C * _softplus(-lam_ref[0][:, sl])

        def gate_item(g, r0):
            sl = slice(g * gw, (g + 1) * gw)
            r = jax.nn.sigmoid(gt_scr[r0:r0 + rc, :gw] + bg_ref[0, 0][:, sl])
            ig = jax.nn.sigmoid(gt_scr[r0:r0 + rc, gw:] + bg_ref[0, 1][:, sl])
            log_a = state["c_row"] * r
            a = jnp.exp(log_a)
            mult = jnp.sqrt(-jnp.tanh(log_a) * (a * a + 1.0))
            a_scr[r0:r0 + rc, sl] = a
            u_scr[r0:r0 + rc, sl] = mult * (ig * xc_scr[r0:r0 + rc, sl])

        items = []
        for g in groups:
            items.append(functools.partial(gate_dot, g))
            items += [functools.partial(gate_item, g, r0) for r0 in chunks]
        return items

    def lru_scan():
        state = {}

        def scan_item(r0, lanes):
            nsub = V7X_SUBLANES
            sub = lax.broadcasted_iota(jnp.int32, (nsub, V7X_LANES), 0)
            key = ("h", lanes.start)
            hprev = hlast_scr[:, lanes] if r0 == 0 else state[key]
            local, decay = [], []
            for r in range(TOK_SEG):
                rows = slice(r0 + r * nsub, r0 + (r + 1) * nsub)
                a = a_scr[rows, lanes]
                u = u_scr[rows, lanes]
                local.append(u if r == 0 else a * local[-1] + u)
                decay.append(a if r == 0 else a * decay[-1])
            ea, eu = decay[-1], local[-1]
            for k in (1, 2, 4):
                keep = sub >= k
                eu = jnp.where(keep, eu + ea * pltpu.roll(eu, k, 0), eu)
                ea = jnp.where(keep, ea * pltpu.roll(ea, k, 0), ea)
            ends = eu + ea * hprev
            h_in = pltpu.roll(jnp.where(sub == nsub - 1, hprev, ends), 1, 0)
            for r in range(TOK_SEG):
                rows = slice(r0 + r * nsub, r0 + (r + 1) * nsub)
                u_scr[rows, lanes] = local[r] + decay[r] * h_in
            hnew = jnp.broadcast_to(ends[nsub - 1:nsub, :], (nsub, V7X_LANES))
            state[key] = hnew
            if r0 + TOK_BLOCK == tm:
                hlast_scr[:, lanes] = hnew

        return [functools.partial(scan_item, r0, lanes) for r0, lanes in runs]

    def lru_out():
        def out_item(r0):
            ya = gelu_scr[r0:r0 + rc, :] * u_scr[r0:r0 + rc, :]
            ya_ref[0, r0:r0 + rc, :] = ya.astype(BF16)
        return [functools.partial(out_item, r0) for r0 in chunks]

    def conv_silu(ci, par):
        def item(r0, lanes):
            half = 0.5 * conv_run(par, ci, r0, lanes)
            seg_ref[0, 0, r0:r0 + TOK_BLOCK, lanes] = (half + half * jnp.tanh(half)).astype(BF16)
        return ([functools.partial(load_carry, par, ci)]
                + [functools.partial(item, r0, lanes) for r0, lanes in runs]
                + [functools.partial(save_carry, par, ci)])

    def plain(par):
        def item(r0):
            seg_ref[0, 0, r0:r0 + rc, :] = pc_scr[par, halo + r0:halo + r0 + rc, :].astype(BF16)
        return [functools.partial(item, r0) for r0 in chunks]

    n_groups = c // gw
    schedule = {
        SEG_GR: lambda par: gate_branch(par),
        SEG_XR: lambda par: lru_conv(par) + lru_gates(range(0, 1)),
        SEG_Q: lambda par: conv_silu(1, par),
        SEG_K: lambda par: conv_silu(2, par),
        SEG_V: lambda par: plain(par) + lru_gates(range(1, n_groups - 1)),
        SEG_O: lambda par: plain(par) + lru_gates(range(n_groups - 1, n_groups)) + lru_scan(),
        SEG_GA: lambda par: plain(par) + lru_out(),
        SEG_GB: lambda par: plain(par),
    }

    @pl.when(m == 0)
    def _():
        carry_scr[...] = jnp.zeros_like(carry_scr)
        hlast_scr[...] = jnp.zeros_like(hlast_scr)
        for ci in range(cwb_scr.shape[0]):
            cwb_scr[ci] = _replicate_rows(jnp.concatenate([cw_ref[0, ci], cb_ref[0, ci]], axis=0))
        norm_and_gates()
        project(0)

    for j in range(N_SEG):
        @pl.when((m > 0) & (seg_done == j))
        def _(j=j):
            par = j % 2
            items = schedule[j](par)
            if j + 1 < N_SEG:
                interleave(1 - par, items)
            else:
                for item in items:
                    item()

                @pl.when(m < n_steps)
                def _():
                    norm_and_gates()
                    project(1 - par)


def _inproj(layer, h, norm_g, w_main, w_ifT, b_if, conv_w, conv_b, w_gate, b_gate, lam, *, tm=1024,
            rc=32, n_blk=4):
    b, s, d = h.shape
    n_groups = d // V7X_MXU_DIM
    halo = (CONV_W - 1) * V7X_SUBLANES
    assert tm % TOK_BLOCK == 0
    n_tiles = s // tm
    n_steps = n_tiles * N_SEG
    kern = functools.partial(_inproj_kernel, tm=tm, rc=rc, n_blk=n_blk, n_steps=n_steps)
    vmem = (2 * tm * d * 4 + 2 * d * d * 2 + 2 * tm * d * 2 + 2 * IFG_ROWS * tm * 4
            + tm * d * 2 + 2 * (tm + halo) * d * 4 + 4 * tm * d * 4 + tm * d * 2
            + 2 * n_groups * V7X_MXU_DIM * 2 * V7X_MXU_DIM * 2 + 3 * tm * d * 4)
    l = layer

    def proj_tile(m):
        return jnp.minimum(m // N_SEG, n_tiles - 1)

    def done_tile(m):
        return jnp.maximum(m - 1, 0) // N_SEG

    def done_slot(m):
        return jnp.maximum(jnp.maximum(m - 1, 0) % N_SEG - FIRST_STORED_SEG, 0)

    return pl.pallas_call(
        kern,
        grid=(b, n_steps + 1),
        in_specs=[
            pl.BlockSpec((1, tm, d), lambda bi, m: (bi, proj_tile(m), 0)),
            pl.BlockSpec((1, 1, d), lambda bi, m: (l, 0, 0)),
            pl.BlockSpec((1, 1, d, d), lambda bi, m: (l, m % N_SEG, 0, 0)),
            pl.BlockSpec((1, IFG_ROWS, d), lambda bi, m: (l, 0, 0)),
            pl.BlockSpec((1, IFG_ROWS, 1), lambda bi, m: (l, 0, 0)),
            pl.BlockSpec((1, 3, CONV_W, d), lambda bi, m: (l, 0, 0, 0)),
            pl.BlockSpec((1, 3, 1, d), lambda bi, m: (l, 0, 0, 0)),
            pl.BlockSpec((1, n_groups, V7X_MXU_DIM, 2 * V7X_MXU_DIM), lambda bi, m: (l, 0, 0, 0)),
            pl.BlockSpec((1, 2, 1, d), lambda bi, m: (l, 0, 0, 0)),
            pl.BlockSpec((1, 1, d), lambda bi, m: (l, 0, 0)),
        ],
        out_specs=[
            pl.BlockSpec((1, 1, tm, d), lambda bi, m: (done_slot(m), bi, done_tile(m), 0)),
            pl.BlockSpec((1, tm, d), lambda bi, m: (bi, done_tile(m), 0)),
            pl.BlockSpec((1, IFG_ROWS, tm), lambda bi, m: (bi, 0, proj_tile(m))),
        ],
        out_shape=[
            jax.ShapeDtypeStruct((N_SLOT, b, s, d), BF16),
            jax.ShapeDtypeStruct((b, s, d), BF16),
            jax.ShapeDtypeStruct((b, IFG_ROWS, s), F32),
        ],
        scratch_shapes=[
            pltpu.VMEM((tm, d), BF16),
            pltpu.VMEM((2, tm + halo, d), F32),
            pltpu.VMEM((tm, d), F32),
            pltpu.VMEM((tm, d), F32),
            pltpu.VMEM((tm, d), BF16),
            pltpu.VMEM((tm, 2 * V7X_MXU_DIM), F32),
            pltpu.VMEM((tm, d), F32),
            pltpu.VMEM((tm, d), F32),
            pltpu.VMEM((3, halo, d), F32),
            pltpu.VMEM((V7X_SUBLANES, d), F32),
            pltpu.VMEM((3, CONV_W + 1, V7X_SUBLANES, d), F32),
        ],
        compiler_params=_params(2, vmem),
        name=f"inproj{layer}",
    )(h, norm_g, w_main, w_ifT, b_if, conv_w, conv_b, w_gate, b_gate, lam)


def _lane_scan(x, lane, op, fill):
    n = x.shape[-1]
    k = 1
    while k < n:
        x = op(x, jnp.where(lane >= k, pltpu.roll(x, k, 1), fill))
        k *= 2
    return x


_DONE = object()


def _mlstm_kernel(q_ref, k_ref, v_ref, ifg_ref, y_ref, c_scr, m_scr):
    @pl.when(pl.program_id(1) == 0)
    def _():
        c_scr[...] = jnp.zeros_like(c_scr)
        m_scr[...] = jnp.zeros_like(m_scr)

    nb = y_ref.shape[0]
    gates = [_mlstm_gates(bb, ifg_ref, m_scr) for bb in range(nb)]
    heads = [_mlstm_head(bb, hd, gates[bb], q_ref, k_ref, v_ref, y_ref, c_scr)
             for hd in range(ML_HEADS) for bb in range(nb)]
    while heads:
        heads = [h for h in heads if next(h, _DONE) is not _DONE]


def _mlstm_gates(bb, ifg_ref, m_scr):
    L = ML_CHUNK
    rows = V7X_SUBLANES

    ifg = ifg_ref[bb]
    li = ifg[0:rows]
    lf = -_softplus(-ifg[rows:2 * rows])
    src = lax.broadcasted_iota(jnp.int32, (L, L), 0)
    dst = lax.broadcasted_iota(jnp.int32, (L, L), 1)
    not_after = (_time_of_row(src) <= _time_of_row(dst)).astype(F32)
    bcum = jnp.dot(lf, not_after, precision=lax.Precision.HIGHEST,
                   preferred_element_type=F32)
    m_st = m_scr[bb]
    g_tot = jnp.broadcast_to(bcum[:, L - 1:L], (rows, L))
    r1 = li - bcum
    u = g_tot - bcum + li
    m_next = jnp.maximum(g_tot + m_st, jnp.max(u, axis=-1, keepdims=True))
    decay = jnp.exp(g_tot + m_st - m_next)
    uexp = jnp.exp(u - m_next)
    m_scr[bb] = m_next

    stacked = jnp.concatenate(
        [bcum, m_st, jnp.zeros((L - 2 * rows, L), F32)], axis=0)
    cols = stacked.T
    return cols, r1, decay, uexp


def _mlstm_head(bb, hd, gates, q_ref, k_ref, v_ref, y_ref, c_scr):
    L = ML_CHUNK
    dh = c_scr.shape[2]
    rows = V7X_SUBLANES
    cols, r1, decay, uexp = gates
    tt = lax.broadcasted_iota(jnp.int32, (L, L), 0)
    ss = lax.broadcasted_iota(jnp.int32, (L, L), 1)
    causal = _time_of_row(ss) <= _time_of_row(tt)

    sl = slice(hd * dh, (hd + 1) * dh)
    qh = q_ref[0, bb, :, sl]
    kh = k_ref[0, bb, :, sl] * jnp.asarray(dh ** -0.5, BF16)
    vh = jnp.concatenate([v_ref[0, bb, :, sl], jnp.ones((L, V7X_LANES), BF16)], axis=1)
    b_col = cols[:, hd:hd + 1]
    a_inter = b_col + cols[:, rows + hd:rows + hd + 1]
    dec = decay[hd:hd + 1, 0:1]
    r1_row = r1[hd:hd + 1, :]

    r1_max = jnp.max(jnp.where(causal, r1_row, -jnp.inf), axis=-1, keepdims=True)
    m_t = jnp.maximum(a_inter, b_col + r1_max)
    qk = lax.dot_general(qh, kh, (((1,), (1,)), ((), ())), preferred_element_type=F32)
    c_old = c_scr[bb, hd]
    qc = jnp.dot(qh, c_old.astype(BF16), preferred_element_type=F32)
    yield
    sc = qk * jnp.exp(jnp.where(causal, (b_col - m_t) + r1_row, -jnp.inf))
    wic = jnp.exp(a_inter - m_t)
    enc = jnp.exp(-m_t)
    both = wic * qc + jnp.dot(sc.astype(BF16), vh, preferred_element_type=F32)
    wk_t = (kh.T.astype(F32) * uexp[hd:hd + 1, :]).astype(BF16)
    c_scr[bb, hd] = dec * c_old + jnp.dot(wk_t, vh, preferred_element_type=F32)
    yield
    num = both[:, :dh]
    den = both[:, dh:dh + 1]
    hh = num * (1.0 / jnp.maximum(jnp.abs(den), enc))
    y_ref[bb, :, sl] = hh.astype(BF16)


def _mlstm(layer, segs, ifg, *, nb=4):
    _, b, s, d = segs.shape
    L = ML_CHUNK
    dh = d // ML_HEADS

    def seg_spec(slot):
        return pl.BlockSpec((1, nb, L, d), lambda bi, ci: (slot, bi, ci, 0))

    vmem = nb * (2 * (3 * L * d * 2 + IFG_ROWS * L * 4 + L * d * 2)
                 + ML_HEADS * dh * (dh + V7X_LANES) * 4 + 16 * L * d * 4)
    return pl.pallas_call(
        _mlstm_kernel,
        grid=(b // nb, s // L),
        in_specs=[
            seg_spec(SLOT_Q), seg_spec(SLOT_K), seg_spec(SLOT_V),
            pl.BlockSpec((nb, IFG_ROWS, L), lambda bi, ci: (bi, 0, ci)),
        ],
        out_specs=pl.BlockSpec((nb, L, d), lambda bi, ci: (bi, ci, 0)),
        out_shape=jax.ShapeDtypeStruct((b, s, d), BF16),
        scratch_shapes=[
            pltpu.VMEM((nb, ML_HEADS, dh, dh + V7X_LANES), F32),
            pltpu.VMEM((nb, V7X_SUBLANES, L), F32),
        ],
        compiler_params=_params(2, vmem),
        name=f"mlstm{layer}",
    )(segs, segs, segs, ifg)


def _merge_kernel(ya_ref, hm_ref, og_ref, ga_ref, gb_ref, h_ref, ng_ref, wa_ref, wb_ref, wm_ref,
                  o_ref, yb_scr, *, n_blk):
    tm, d = yb_scr.shape
    dh = d // ML_HEADS
    bm = tm // n_blk
    for blk in range(n_blk):
        rows = slice(blk * bm, (blk + 1) * bm)
        for r0 in range(blk * bm, (blk + 1) * bm, TOK_BLOCK):
            for hd in range(ML_HEADS):
                sl = slice(hd * dh, (hd + 1) * dh)
                hh = hm_ref[0, r0:r0 + TOK_BLOCK, sl].astype(F32)
                mu = jnp.mean(hh, axis=-1, keepdims=True)
                cen = hh - mu
                var = jnp.mean(cen * cen, axis=-1, keepdims=True)
                hn = (cen * lax.rsqrt(var + EPS)) * ng_ref[0][:, sl]
                gate = jax.nn.sigmoid(og_ref[0, 0, r0:r0 + TOK_BLOCK, sl].astype(F32))
                yb_scr[r0:r0 + TOK_BLOCK, sl] = (gate * hn).astype(BF16)
        pa = jnp.dot(ya_ref[0, rows, :], wa_ref[0], preferred_element_type=F32)
        pb = jnp.dot(yb_scr[rows, :], wb_ref[0], preferred_element_type=F32)
        y = (jax.nn.sigmoid(ga_ref[0, 0, rows, :].astype(F32)) * pa
             + jax.nn.sigmoid(gb_ref[0, 0, rows, :].astype(F32)) * pb)
        o_ref[0, rows, :] = h_ref[0, rows, :] + jnp.dot(y.astype(BF16), wm_ref[0],
                                                        preferred_element_type=F32)


def _merge(layer, ya, hm, segs, h, ml_norm_g, wa, wb, wm, *, tm=512, n_blk=2):
    b, s, d = h.shape
    l = layer

    def seg_spec(slot):
        return pl.BlockSpec((1, 1, tm, d), lambda bi, i: (slot, bi, i, 0))

    w_spec = pl.BlockSpec((1, d, d), lambda bi, i: (l, 0, 0))
    row_spec = pl.BlockSpec((1, tm, d), lambda bi, i: (bi, i, 0))
    vmem = 2 * (5 * tm * d * 2 + 2 * tm * d * 4 + 3 * d * d * 2) + tm * d * 2 + 4 * tm * d * 4
    return pl.pallas_call(
        functools.partial(_merge_kernel, n_blk=n_blk),
        grid=(b, s // tm),
        in_specs=[row_spec, row_spec, seg_spec(SLOT_O), seg_spec(SLOT_GA), seg_spec(SLOT_GB),
                  row_spec, pl.BlockSpec((1, 1, d), lambda bi, i: (l, 0, 0)),
                  w_spec, w_spec, w_spec],
        out_specs=row_spec,
        out_shape=jax.ShapeDtypeStruct((b, s, d), F32),
        scratch_shapes=[pltpu.VMEM((tm, d), BF16)],
        compiler_params=_params(2, vmem),
        name=f"merge{layer}",
    )(ya, hm, segs, segs, segs, h, ml_norm_g, wa, wb, wm)


def _xattn_kernel(h_ref, g_ref, wq_ref, kv_ref, wo_ref, o_ref, q_scr, att_scr, *, n_blk):
    tm, d = att_scr.shape
    dh = d // XA_HEADS
    bm = tm // n_blk
    scale = dh ** -0.5

    def head(rows, hd):
        sl = slice(hd * dh, (hd + 1) * dh)
        kh = kv_ref[0, 0, :, sl]
        vh = kv_ref[0, 0, :, d + hd * dh:d + (hd + 1) * dh]
        sc = lax.dot_general(q_scr[rows, sl], kh, (((1,), (1,)), ((), ())),
                             preferred_element_type=F32) * scale
        top = jnp.max(sc, axis=-1, keepdims=True)
        yield
        e = jnp.exp(sc - top)
        den = jnp.sum(e, axis=-1, keepdims=True)
        yield
        p = e * (1.0 / den)
        att_scr[rows, sl] = jnp.dot(p.astype(BF16), vh, preferred_element_type=F32).astype(BF16)

    for blk in range(n_blk):
        rows = slice(blk * bm, (blk + 1) * bm)
        xn = _rms_norm(h_ref[0, rows, :], g_ref[0]).astype(BF16)
        q_scr[rows, :] = jnp.dot(xn, wq_ref[0], preferred_element_type=F32).astype(BF16)
        heads = [head(rows, hd) for hd in range(XA_HEADS)]
        while heads:
            heads = [h for h in heads if next(h, _DONE) is not _DONE]
        o_ref[0, rows, :] = h_ref[0, rows, :] + jnp.dot(att_scr[rows, :], wo_ref[0],
                                                        preferred_element_type=F32)


def _xattn(layer, h, norm_g, wq, kv, wo, *, tm=512, n_blk=2):
    b, s, d = h.shape
    m = kv.shape[2]
    l = layer
    row_spec = pl.BlockSpec((1, tm, d), lambda bi, i: (bi, i, 0))
    w_spec = pl.BlockSpec((1, d, d), lambda bi, i: (l, 0, 0))
    vmem = 2 * (2 * tm * d * 4 + 2 * d * d * 2 + m * 2 * d * 2) + 2 * tm * d * 2 + 6 * tm * d * 4
    return pl.pallas_call(
        functools.partial(_xattn_kernel, n_blk=n_blk),
        grid=(b, s // tm),
        in_specs=[
            row_spec,
            pl.BlockSpec((1, 1, d), lambda bi, i: (l, 0, 0)),
            w_spec,
            pl.BlockSpec((1, 1, m, 2 * d), lambda bi, i: (l, bi, 0, 0)),
            w_spec,
        ],
        out_specs=row_spec,
        out_shape=jax.ShapeDtypeStruct((b, s, d), F32),
        scratch_shapes=[pltpu.VMEM((tm, d), BF16),
                        pltpu.VMEM((tm, d), BF16)],
        compiler_params=_params(2, vmem),
        name=f"xattn{layer}",
    )(h, norm_g, wq, kv, wo)


def _ffn_kernel(x_ref, g_ref, wg_ref, wu_ref, cwg_ref, cwu_ref, cbg_ref, cbu_ref, wd_ref, fg_ref,
                o_ref, xn_scr, acc_scr, pc_scr, act_scr, carry_scr, cwb_scr, *, tm, rc, n_j, n_blk,
                final_norm):
    i = pl.program_id(1)
    j = pl.program_id(2)
    halo = (FFN_CONV_W - 1) * V7X_SUBLANES
    tf = act_scr.shape[1]

    @pl.when(j == 0)
    def _():
        xn_scr[...] = _rms_norm(x_ref[0], g_ref[0]).astype(BF16)
        acc_scr[...] = jnp.zeros_like(acc_scr)

    @pl.when(i == 0)
    def _():
        carry_scr[j] = jnp.zeros(carry_scr.shape[1:], F32)

    for half in range(2):
        pc_scr[half, 0:halo, :] = carry_scr[j, half]

    for half, (cw_ref, cb_ref) in enumerate(((cwg_ref, cbg_ref), (cwu_ref, cbu_ref))):
        cwb_scr[half] = _replicate_rows(jnp.concatenate([cw_ref[0], cb_ref[0]], axis=0))

    def conv_run(half, r0, lanes):
        base = halo + r0
        return _causal_conv_run(pc_scr[half, base - halo:base, lanes],
                                pc_scr[half, base:base + TOK_BLOCK, lanes],
                                cwb_scr[half, :, :, lanes])

    bm = tm // n_blk
    d = acc_scr.shape[1]
    pw = V7X_MXU_DIM

    def up_pieces(b):
        def piece(half, w_ref, n0):
            pc_scr[half, halo + b * bm:halo + (b + 1) * bm, n0:n0 + pw] = jnp.dot(
                xn_scr[b * bm:(b + 1) * bm, :], w_ref[0, 0, :, n0:n0 + pw],
                preferred_element_type=F32)
        return [functools.partial(piece, half, w_ref, n0)
                for half, w_ref in enumerate((wg_ref, wu_ref)) for n0 in range(0, tf, pw)]

    def epilogue_items(b):
        def item(r0, lanes):
            gpre = conv_run(0, r0, lanes)
            upre = conv_run(1, r0, lanes)
            act_scr[r0:r0 + TOK_BLOCK, lanes] = (_gelu_tanh(gpre) * upre).astype(BF16)
        return [functools.partial(item, r0, slice(l0, l0 + V7X_LANES))
                for r0 in range(b * bm, (b + 1) * bm, TOK_BLOCK)
                for l0 in range(0, tf, V7X_LANES)]

    def down_pieces(b):
        rows = slice(b * bm, (b + 1) * bm)

        def piece(n0):
            acc_scr[rows, n0:n0 + pw] += jnp.dot(act_scr[rows, :], wd_ref[0, :, n0:n0 + pw],
                                                 preferred_element_type=F32)
        return [functools.partial(piece, n0) for n0 in range(0, d, pw)]

    def interleave(major, minor):
        n, k = len(minor), len(major)
        for i, piece in enumerate(major):
            piece()
            for item in minor[i * n // k:(i + 1) * n // k]:
                item()

    interleave(up_pieces(0), [])
    for b in range(n_blk):
        matmuls = (up_pieces(b + 1) if b + 1 < n_blk else []) + (down_pieces(b - 1) if b else [])
        interleave(matmuls, epilogue_items(b))
    interleave(down_pieces(n_blk - 1), [])

    for half in range(2):
        carry_scr[j, half] = pc_scr[half, tm:tm + halo, :]

    @pl.when(j == n_j - 1)
    def _():
        oc = 8 * rc

        def out_body(r0):
            res = x_ref[0, pl.ds(r0, oc), :] + acc_scr[pl.ds(r0, oc), :]
            if final_norm:
                res = _rms_norm(res, fg_ref[...])
            o_ref[0, pl.ds(r0, oc), :] = res
        _row_chunks(tm, oc, out_body)


def _ffn(layer, h, norm_g, w_up, conv_w, conv_b, w_down, final_g, *, final_norm, tm=1024, rc=32,
         n_blk=2):
    b, s, d = h.shape
    tf = w_up.shape[3]
    n_j = w_up.shape[1] // 2
    halo = (FFN_CONV_W - 1) * V7X_SUBLANES
    assert (tm // n_blk) % TOK_BLOCK == 0
    l = layer
    kern = functools.partial(_ffn_kernel, tm=tm, rc=rc, n_j=n_j, n_blk=n_blk,
                             final_norm=final_norm)
    row_spec = pl.BlockSpec((1, tm, d), lambda bi, i, j: (bi, i, 0))
    vmem = (4 * tm * d * 4 + 2 * (2 * d * tf * 2 + tf * d * 2) + tm * d * 2 + tm * d * 4
            + 2 * (tm + halo) * tf * 4 + tm * tf * 2 + n_j * 2 * halo * tf * 4 + 2 * tm * tf * 4)
    return pl.pallas_call(
        kern,
        grid=(b, s // tm, n_j),
        in_specs=[
            row_spec,
            pl.BlockSpec((1, 1, d), lambda bi, i, j: (l, 0, 0)),
            pl.BlockSpec((1, 1, d, tf), lambda bi, i, j: (l, j, 0, 0)),
            pl.BlockSpec((1, 1, d, tf), lambda bi, i, j: (l, n_j + j, 0, 0)),
            pl.BlockSpec((1, FFN_CONV_W, tf), lambda bi, i, j: (l, 0, j)),
            pl.BlockSpec((1, FFN_CONV_W, tf), lambda bi, i, j: (l, 0, n_j + j)),
            pl.BlockSpec((1, 1, tf), lambda bi, i, j: (l, 0, j)),
            pl.BlockSpec((1, 1, tf), lambda bi, i, j: (l, 0, n_j + j)),
            pl.BlockSpec((1, tf, d), lambda bi, i, j: (l, j, 0)),
            pl.BlockSpec((1, d), lambda bi, i, j: (0, 0)),
        ],
        out_specs=row_spec,
        out_shape=jax.ShapeDtypeStruct((b, s, d), F32),
        scratch_shapes=[
            pltpu.VMEM((tm, d), BF16),
            pltpu.VMEM((tm, d), F32),
            pltpu.VMEM((2, tm + halo, tf), F32),
            pltpu.VMEM((tm, tf), BF16),
            pltpu.VMEM((n_j, 2, halo, tf), F32),
            pltpu.VMEM((2, FFN_CONV_W + 1, V7X_SUBLANES, tf), F32),
        ],
        compiler_params=_params(3, vmem),
        name=f"ffn{layer}",
    )(h, norm_g, w_up, w_up, conv_w, conv_w, conv_b, conv_b, w_down, final_g)


def _block_diag_groups(w):
    depth, n_blocks, bs, _ = w.shape
    per = V7X_MXU_DIM // bs
    wg = w.reshape(depth, n_blocks // per, per, bs, bs)
    eye = jnp.eye(per, dtype=w.dtype)
    out = wg[:, :, :, :, None, :] * eye[None, None, :, None, :, None]
    return out.reshape(depth, n_blocks // per, per * bs, per * bs)


def kernel(x, mem, norm_mix_g, w_in, rnn_conv_w, rnn_conv_b, lru_wa, lru_ba, lru_wx, lru_bx, lru_lambda, ml_conv_w, ml_conv_b, ml_if_b, ml_norm_g, w_branch_a, w_branch_b, w_mix_out, norm_xa_g, xa_wq, xa_wkv, xa_wo, norm_ffn_g, ffn_w_up, ffn_conv_w, ffn_conv_b, ffn_w_down, mem_norm_g, final_norm_g):
    depth, d, _ = w_in.shape
    d_ml = ml_norm_g.shape[1]
    assert d == d_ml == lru_lambda.shape[1], "kernels assume D_RNN == D_ML == D_MODEL"
    assert ML_HEADS <= V7X_SUBLANES and x.shape[1] % ML_CHUNK == 0

    o_xr, o_gr, o_q, o_k, o_v, o_o = (n * d for n in range(6))
    o_if = 6 * d
    o_ga = o_if + 2 * ML_HEADS
    o_gb = o_ga + d

    def cols(o):
        return w_in[:, :, o:o + d]

    w_main = jnp.stack(
        [cols(o_gr), cols(o_xr), cols(o_q), cols(o_k), cols(o_v), cols(o_o), cols(o_ga), cols(o_gb)],
        axis=1).astype(BF16)
    w_if = jnp.swapaxes(w_in[:, :, o_if:o_ga], 1, 2)
    pad = jnp.zeros((depth, V7X_SUBLANES - ML_HEADS, d), w_in.dtype)
    w_ifT = jnp.concatenate([w_if[:, :ML_HEADS], pad, w_if[:, ML_HEADS:], pad], axis=1).astype(BF16)
    bpad = jnp.zeros((depth, V7X_SUBLANES - ML_HEADS), ml_if_b.dtype)
    b_if = jnp.concatenate([ml_if_b[:, :ML_HEADS], bpad, ml_if_b[:, ML_HEADS:], bpad], axis=1)[..., None]

    conv_w = jnp.stack([rnn_conv_w, ml_conv_w[:, :, :d], ml_conv_w[:, :, d:]], axis=1)
    conv_b = jnp.stack([rnn_conv_b, ml_conv_b[:, :d], ml_conv_b[:, d:]], axis=1)[:, :, None, :]
    w_gate = jnp.concatenate([_block_diag_groups(lru_wa), _block_diag_groups(lru_wx)], axis=-1).astype(BF16)
    b_gate = jnp.stack([lru_ba, lru_bx], axis=1)[:, :, None, :]

    def row(g):
        return g[:, None, :]

    wa, wb, wm = (w.astype(BF16) for w in (w_branch_a, w_branch_b, w_mix_out))
    wq, wkv, wo = (w.astype(BF16) for w in (xa_wq, xa_wkv, xa_wo))
    w_down = ffn_w_down.astype(BF16)
    w_up = ffn_w_up.reshape(depth, d, -1, FFN_TILE)
    w_up = jnp.swapaxes(w_up, 1, 2).astype(BF16)
    ffn_cb = row(ffn_conv_b)

    kv = _memkv(mem, mem_norm_g[None, :], wkv)
    h = _to_block_order(x)
    for l in range(depth):
        segs, ya, ifg = _inproj(l, h, row(norm_mix_g), w_main, w_ifT, b_if, conv_w, conv_b, w_gate,
                                b_gate, row(lru_lambda))
        hm = _mlstm(l, segs, ifg)
        h = _merge(l, ya, hm, segs, h, row(ml_norm_g), wa, wb, wm)
        h = _xattn(l, h, row(norm_xa_g), wq, kv, wo)
        h = _ffn(l, h, row(norm_ffn_g), w_up, ffn_conv_w, ffn_cb, w_down, final_norm_g[None, :],
                 final_norm=(l == depth - 1))
    return _from_block_order(h)
```

```python
import functools

import jax
import jax.numpy as jnp
from jax import lax
from jax.experimental import pallas as pl
from jax.experimental.pallas import tpu as pltpu

F32 = jnp.float32
BF16 = jnp.bfloat16

EPS = 1e-6
LRU_C = 8.0
CONV_W = 4
FFN_CONV_W = 3
RNN_BLOCK = 64
ML_HEADS = 4
ML_CHUNK = 128
XA_HEADS = 4
FFN_TILE = 1024

V7X_SUBLANES = 8
V7X_LANES = 128
V7X_MXU_DIM = 256
V7X_VMEM_BYTES = 64 * 1024 * 1024
V7X_VMEM_REQUEST_CAP = 56 * 1024 * 1024

SEG_GR, SEG_XR, SEG_Q, SEG_K, SEG_V, SEG_O, SEG_GA, SEG_GB = range(8)
SLOT_Q, SLOT_K, SLOT_V, SLOT_O, SLOT_GA, SLOT_GB = range(6)
N_SEG = 8
N_SLOT = 6
FIRST_STORED_SEG = SEG_Q
IFG_ROWS = 16


def _vmem_limit(n_bytes):
    return int(min(V7X_VMEM_REQUEST_CAP, n_bytes + n_bytes // 2 + (4 << 20)))


def _params(n_grid, vmem_bytes):
    return pltpu.CompilerParams(
        dimension_semantics=("arbitrary",) * n_grid,
        vmem_limit_bytes=_vmem_limit(vmem_bytes),
    )


def _rms_norm(xf, g):
    ms = jnp.mean(xf * xf, axis=-1, keepdims=True)
    return (xf * lax.rsqrt(ms + EPS)) * g


def _gelu_tanh(x):
    c0 = (2.0 / jnp.pi) ** 0.5
    c1 = 0.044715 * c0
    half = 0.5 * x
    return half + half * jnp.tanh(x * (c0 + c1 * (x * x)))


def _softplus(z):
    return jnp.maximum(z, 0.0) + jnp.log1p(jnp.exp(-jnp.abs(z)))


def _row_chunks(n_rows, chunk, body):
    assert n_rows % chunk == 0

    def step(r, carry):
        body(pl.multiple_of(r * chunk, chunk))
        return carry

    lax.fori_loop(0, n_rows // chunk, step, 0)


TOK_BLOCK = ML_CHUNK
TOK_SEG = TOK_BLOCK // V7X_SUBLANES


def _to_block_order(t):
    b, s, d = t.shape
    t = t.reshape(b, s // TOK_BLOCK, V7X_SUBLANES, TOK_SEG, d)
    return jnp.swapaxes(t, 2, 3).reshape(b, s, d)


def _from_block_order(t):
    b, s, d = t.shape
    t = t.reshape(b, s // TOK_BLOCK, TOK_SEG, V7X_SUBLANES, d)
    return jnp.swapaxes(t, 2, 3).reshape(b, s, d)


def _time_of_row(row):
    return (row % V7X_SUBLANES) * TOK_SEG + row // V7X_SUBLANES


def _row_of_time(tau):
    return (tau % TOK_SEG) * V7X_SUBLANES + tau // TOK_SEG


def _halo_vregs(prev_tail, cur_tail):
    sub = lax.broadcasted_iota(jnp.int32, (V7X_SUBLANES, cur_tail.shape[1]), 0)
    out = []
    for i in range(cur_tail.shape[0] // V7X_SUBLANES):
        rows = slice(i * V7X_SUBLANES, (i + 1) * V7X_SUBLANES)
        mixed = jnp.where(sub == V7X_SUBLANES - 1, prev_tail[rows], cur_tail[rows])
        out.append(pltpu.roll(mixed, 1, 0))
    return out


def _replicate_rows(rows_1w):
    k, w = rows_1w.shape
    return jnp.broadcast_to(rows_1w[:, None, :], (k, V7X_SUBLANES, w))


def _causal_conv_run(prev_tail, cur, cwb):
    taps = cwb.shape[0] - 1
    nsub = V7X_SUBLANES
    n, w = cur.shape
    tail = (taps - 1) * nsub
    ext = jnp.concatenate(_halo_vregs(prev_tail, cur[n - tail:]) + [cur], axis=0)
    ext = ext.reshape(ext.shape[0] // nsub, nsub, w)
    out = cwb[taps][None]
    for k in range(taps):
        out = out + ext[k:k + n // nsub] * cwb[k][None]
    return out.reshape(n, w)


def _memkv_kernel(mem_ref, g_ref, w_ref, o_ref):
    memn = _rms_norm(mem_ref[0], g_ref[...]).astype(BF16)
    o_ref[0, 0] = jnp.dot(memn, w_ref[0], preferred_element_type=F32).astype(BF16)


def _memkv(mem, mem_g, wkv):
    depth, d, n = wkv.shape
    b, m, _ = mem.shape
    vmem = 2 * (m * d * 4 + d * n * 2 + m * n * 2) + m * n * 4
    return pl.pallas_call(
        _memkv_kernel,
        grid=(depth, b),
        in_specs=[
            pl.BlockSpec((1, m, d), lambda l, i: (i, 0, 0)),
            pl.BlockSpec((1, d), lambda l, i: (0, 0)),
            pl.BlockSpec((1, d, n), lambda l, i: (l, 0, 0)),
        ],
        out_specs=pl.BlockSpec((1, 1, m, n), lambda l, i: (l, i, 0, 0)),
        out_shape=jax.ShapeDtypeStruct((depth, b, m, n), BF16),
        compiler_params=_params(2, vmem),
        name="memkv",
    )(mem, mem_g, wkv)


def _inproj_kernel(x_ref, g_ref, w_ref, wif_ref, bif_ref, cw_ref, cb_ref, wg_ref, bg_ref,
                   lam_ref, seg_ref, ya_ref, ifg_ref,
                   xn_scr, pc_scr, gelu_scr, xc_scr, xcb_scr, gt_scr, a_scr, u_scr,
                   carry_scr, hlast_scr, cwb_scr, *, tm, rc, n_blk, n_steps):
    m = pl.program_id(1)
    seg_done = (m + N_SEG - 1) % N_SEG
    c = xn_scr.shape[1]
    halo = (CONV_W - 1) * V7X_SUBLANES
    gw = V7X_MXU_DIM
    chunks = range(0, tm, rc)
    runs = [(r0, slice(l0, l0 + V7X_LANES))
            for r0 in range(0, tm, TOK_BLOCK) for l0 in range(0, c, V7X_LANES)]

    def norm_and_gates():
        xn = _rms_norm(x_ref[0], g_ref[0]).astype(BF16)
        xn_scr[...] = xn
        ifg = lax.dot_general(wif_ref[0], xn, (((1,), (1,)), ((), ())),
                              preferred_element_type=F32)
        ifg_ref[0] = ifg + bif_ref[0]

    bm = tm // n_blk

    pieces = [(b, n0) for b in range(n_blk) for n0 in range(0, c, gw)]

    def project_piece(par, b, n0):
        pc_scr[par, halo + b * bm:halo + (b + 1) * bm, n0:n0 + gw] = jnp.dot(
            xn_scr[b * bm:(b + 1) * bm, :], w_ref[0, 0, :, n0:n0 + gw],
            preferred_element_type=F32)

    def project(par):
        for b, n0 in pieces:
            project_piece(par, b, n0)

    def interleave(par_next, items):
        n, k = len(items), len(pieces)
        for i, (b, n0) in enumerate(pieces):
            project_piece(par_next, b, n0)
            for item in items[i * n // k:(i + 1) * n // k]:
                item()

    def conv_run(par, ci, r0, lanes):
        base = halo + r0
        return _causal_conv_run(pc_scr[par, base - halo:base, lanes],
                                pc_scr[par, base:base + TOK_BLOCK, lanes],
                                cwb_scr[ci, :, :, lanes])

    def load_carry(par, ci):
        pc_scr[par, 0:halo, :] = carry_scr[ci]

    def save_carry(par, ci):
        carry_scr[ci] = pc_scr[par, tm:tm + halo, :]

    def gate_branch(par):
        def item(r0):
            gelu_scr[r0:r0 + rc, :] = _gelu_tanh(pc_scr[par, halo + r0:halo + r0 + rc, :])
        return [functools.partial(item, r0) for r0 in chunks]

    def lru_conv(par):
        def conv_item(r0, lanes):
            xc = conv_run(par, 0, r0, lanes)
            xc_scr[r0:r0 + TOK_BLOCK, lanes] = xc
            xcb_scr[r0:r0 + TOK_BLOCK, lanes] = xc.astype(BF16)
        return ([functools.partial(load_carry, par, 0)]
                + [functools.partial(conv_item, r0, lanes) for r0, lanes in runs]
                + [functools.partial(save_carry, par, 0)])

    def lru_gates(groups):
        state = {}

        def gate_dot(g):
            sl = slice(g * gw, (g + 1) * gw)
            gt_scr[...] = jnp.dot(xcb_scr[:, sl], wg_ref[0, g], preferred_element_type=F32)
            state["c_row"] = -LRU_C * _softplus(-lam_ref[0][:, sl])

        def gate_item(g, r0):
            sl = slice(g * gw, (g + 1) * gw)
            r = jax.nn.sigmoid(gt_scr[r0:r0 + rc, :gw] + bg_ref[0, 0][:, sl])
            ig = jax.nn.sigmoid(gt_scr[r0:r0 + rc, gw:] + bg_ref[0, 1][:, sl])
            log_a = state["c_row"] * r
            a = jnp.exp(log_a)
            mult = jnp.sqrt(-jnp.tanh(log_a) * (a * a + 1.0))
            a_scr[r0:r0 + rc, sl] = a
            u_scr[r0:r0 + rc, sl] = mult * (ig * xc_scr[r0:r0 + rc, sl])

        items = []
        for g in groups:
            items.append(functools.partial(gate_dot, g))
            items += [functools.partial(gate_item, g, r0) for r0 in chunks]
        return items

    def lru_scan():
        state = {}

        def scan_item(r0, lanes):
            nsub = V7X_SUBLANES
            sub = lax.broadcasted_iota(jnp.int32, (nsub, V7X_LANES), 0)
            key = ("h", lanes.start)
            hprev = hlast_scr[:, lanes] if r0 == 0 else state[key]
            local, decay = [], []
            for r in range(TOK_SEG):
                rows = slice(r0 + r * nsub, r0 + (r + 1) * nsub)
                a = a_scr[rows, lanes]
                u = u_scr[rows, lanes]
                local.append(u if r == 0 else a * local[-1] + u)
                decay.append(a if r == 0 else a * decay[-1])
            ea, eu = decay[-1], local[-1]
            for k in (1, 2, 4):
                keep = sub >= k
                eu = jnp.where(keep, eu + ea * pltpu.roll(eu, k, 0), eu)
                ea = jnp.where(keep, ea * pltpu.roll(ea, k, 0), ea)
            ends = eu + ea * hprev
            h_in = pltpu.roll(jnp.where(sub == nsub - 1, hprev, ends), 1, 0)
            for r in range(TOK_SEG):
                rows = slice(r0 + r * nsub, r0 + (r + 1) * nsub)
                u_scr[rows, lanes] = local[r] + decay[r] * h_in
            hnew = jnp.broadcast_to(ends[nsub - 1:nsub, :], (nsub, V7X_LANES))
            state[key] = hnew
            if r0 + TOK_BLOCK == tm:
                hlast_scr[:, lanes] = hnew

        return [functools.partial(scan_item, r0, lanes) for r0, lanes in runs]

    def lru_out():
        def out_item(r0):
            ya = gelu_scr[r0:r0 + rc, :] * u_scr[r0:r0 + rc, :]
            ya_ref[0, r0:r0 + rc, :] = ya.astype(BF16)
        return [functools.partial(out_item, r0) for r0 in chunks]

    def conv_silu(ci, par):
        def item(r0, lanes):
            half = 0.5 * conv_run(par, ci, r0, lanes)
            seg_ref[0, 0, r0:r0 + TOK_BLOCK, lanes] = (half + half * jnp.tanh(half)).astype(BF16)
        return ([functools.partial(load_carry, par, ci)]
                + [functools.partial(item, r0, lanes) for r0, lanes in runs]
                + [functools.partial(save_carry, par, ci)])

    def plain(par):
        def item(r0):
            seg_ref[0, 0, r0:r0 + rc, :] = pc_scr[par, halo + r0:halo + r0 + rc, :].astype(BF16)
        return [functools.partial(item, r0) for r0 in chunks]

    n_groups = c // gw
    schedule = {
        SEG_GR: lambda par: gate_branch(par),
        SEG_XR: lambda par: lru_conv(par) + lru_gates(range(0, 1)),
        SEG_Q: lambda par: conv_silu(1, par),
        SEG_K: lambda par: conv_silu(2, par),
        SEG_V: lambda par: plain(par) + lru_gates(range(1, n_groups - 1)),
        SEG_O: lambda par: plain(par) + lru_gates(range(n_groups - 1, n_groups)) + lru_scan(),
        SEG_GA: lambda par: plain(par) + lru_out(),
        SEG_GB: lambda par: plain(par),
    }

    @pl.when(m == 0)
    def _():
        carry_scr[...] = jnp.zeros_like(carry_scr)
        hlast_scr[...] = jnp.zeros_like(hlast_scr)
        for ci in range(cwb_scr.shape[0]):
            cwb_scr[ci] = _replicate_rows(jnp.concatenate([cw_ref[0, ci], cb_ref[0, ci]], axis=0))
        norm_and_gates()
        project(0)

    for j in range(N_SEG):
        @pl.when((m > 0) & (seg_done == j))
        def _(j=j):
            par = j % 2
            items = schedule[j](par)
            if j + 1 < N_SEG:
                interleave(1 - par, items)
            else:
                for item in items:
                    item()

                @pl.when(m < n_steps)
                def _():
                    norm_and_gates()
                    project(1 - par)


def _inproj(layer, h, norm_g, w_main, w_ifT, b_if, conv_w, conv_b, w_gate, b_gate, lam, *, tm=1024,
            rc=32, n_blk=2):
    b, s, d = h.shape
    n_groups = d // V7X_MXU_DIM
    halo = (CONV_W - 1) * V7X_SUBLANES
    assert tm % TOK_BLOCK == 0
    n_tiles = s // tm
    n_steps = n_tiles * N_SEG
    kern = functools.partial(_inproj_kernel, tm=tm, rc=rc, n_blk=n_blk, n_steps=n_steps)
    vmem = (2 * tm * d * 4 + 2 * d * d * 2 + 2 * tm * d * 2 + 2 * IFG_ROWS * tm * 4
            + tm * d * 2 + 2 * (tm + halo) * d * 4 + 4 * tm * d * 4 + tm * d * 2
            + 2 * n_groups * V7X_MXU_DIM * 2 * V7X_MXU_DIM * 2 + 3 * tm * d * 4)
    l = layer

    def proj_tile(m):
        return jnp.minimum(m // N_SEG, n_tiles - 1)

    def done_tile(m):
        return jnp.maximum(m - 1, 0) // N_SEG

    def done_slot(m):
        return jnp.maximum(jnp.maximum(m - 1, 0) % N_SEG - FIRST_STORED_SEG, 0)

    return pl.pallas_call(
        kern,
        grid=(b, n_steps + 1),
        in_specs=[
            pl.BlockSpec((1, tm, d), lambda bi, m: (bi, proj_tile(m), 0)),
            pl.BlockSpec((1, 1, d), lambda bi, m: (l, 0, 0)),
            pl.BlockSpec((1, 1, d, d), lambda bi, m: (l, m % N_SEG, 0, 0)),
            pl.BlockSpec((1, IFG_ROWS, d), lambda bi, m: (l, 0, 0)),
            pl.BlockSpec((1, IFG_ROWS, 1), lambda bi, m: (l, 0, 0)),
            pl.BlockSpec((1, 3, CONV_W, d), lambda bi, m: (l, 0, 0, 0)),
            pl.BlockSpec((1, 3, 1, d), lambda bi, m: (l, 0, 0, 0)),
            pl.BlockSpec((1, n_groups, V7X_MXU_DIM, 2 * V7X_MXU_DIM), lambda bi, m: (l, 0, 0, 0)),
            pl.BlockSpec((1, 2, 1, d), lambda bi, m: (l, 0, 0, 0)),
            pl.BlockSpec((1, 1, d), lambda bi, m: (l, 0, 0)),
        ],
        out_specs=[
            pl.BlockSpec((1, 1, tm, d), lambda bi, m: (done_slot(m), bi, done_tile(m), 0)),
            pl.BlockSpec((1, tm, d), lambda bi, m: (bi, done_tile(m), 0)),
            pl.BlockSpec((1, IFG_ROWS, tm), lambda bi, m: (bi, 0, proj_tile(m))),
        ],
        out_shape=[
            jax.ShapeDtypeStruct((N_SLOT, b, s, d), BF16),
            jax.ShapeDtypeStruct((b, s, d), BF16),
            jax.ShapeDtypeStruct((b, IFG_ROWS, s), F32),
        ],
        scratch_shapes=[
            pltpu.VMEM((tm, d), BF16),
            pltpu.VMEM((2, tm + halo, d), F32),
            pltpu.VMEM((tm, d), F32),
            pltpu.VMEM((tm, d), F32),
            pltpu.VMEM((tm, d), BF16),
            pltpu.VMEM((tm, 2 * V7X_MXU_DIM), F32),
            pltpu.VMEM((tm, d), F32),
            pltpu.VMEM((tm, d), F32),
            pltpu.VMEM((3, halo, d), F32),
            pltpu.VMEM((V7X_SUBLANES, d), F32),
            pltpu.VMEM((3, CONV_W + 1, V7X_SUBLANES, d), F32),
        ],
        compiler_params=_params(2, vmem),
        name=f"inproj{layer}",
    )(h, norm_g, w_main, w_ifT, b_if, conv_w, conv_b, w_gate, b_gate, lam)


def _lane_scan(x, lane, op, fill):
    n = x.shape[-1]
    k = 1
    while k < n:
        x = op(x, jnp.where(lane >= k, pltpu.roll(x, k, 1), fill))
        k *= 2
    return x


_DONE = object()


def _mlstm_kernel(q_ref, k_ref, v_ref, ifg_ref, y_ref, c_scr, m_scr):
    @pl.when(pl.program_id(1) == 0)
    def _():
        c_scr[...] = jnp.zeros_like(c_scr)
        m_scr[...] = jnp.zeros_like(m_scr)

    nb = y_ref.shape[0]
    gates = [_mlstm_gates(bb, ifg_ref, m_scr) for bb in range(nb)]
    heads = [_mlstm_head(bb, hd, gates[bb], q_ref, k_ref, v_ref, y_ref, c_scr)
             for hd in range(ML_HEADS) for bb in range(nb)]
    while heads:
        heads = [h for h in heads if next(h, _DONE) is not _DONE]


def _mlstm_gates(bb, ifg_ref, m_scr):
    L = ML_CHUNK
    rows = V7X_SUBLANES

    ifg = ifg_ref[bb]
    li = ifg[0:rows]
    lf = -_softplus(-ifg[rows:2 * rows])
    src = lax.broadcasted_iota(jnp.int32, (L, L), 0)
    dst = lax.broadcasted_iota(jnp.int32, (L, L), 1)
    not_after = (_time_of_row(src) <= _time_of_row(dst)).astype(F32)
    bcum = jnp.dot(lf, not_after, precision=lax.Precision.HIGHEST,
                   preferred_element_type=F32)
    m_st = m_scr[bb]
    g_tot = jnp.broadcast_to(bcum[:, L - 1:L], (rows, L))
    r1 = li - bcum
    u = g_tot - bcum + li
    m_next = jnp.maximum(g_tot + m_st, jnp.max(u, axis=-1, keepdims=True))
    decay = jnp.exp(g_tot + m_st - m_next)
    uexp = jnp.exp(u - m_next)
    m_scr[bb] = m_next

    stacked = jnp.concatenate(
        [bcum, m_st, jnp.zeros((L - 2 * rows, L), F32)], axis=0)
    cols = stacked.T
    return cols, r1, decay, uexp


def _mlstm_head(bb, hd, gates, q_ref, k_ref, v_ref, y_ref, c_scr):
    L = ML_CHUNK
    dh = c_scr.shape[2]
    rows = V7X_SUBLANES
    cols, r1, decay, uexp = gates
    tt = lax.broadcasted_iota(jnp.int32, (L, L), 0)
    ss = lax.broadcasted_iota(jnp.int32, (L, L), 1)
    causal = _time_of_row(ss) <= _time_of_row(tt)

    sl = slice(hd * dh, (hd + 1) * dh)
    qh = q_ref[0, bb, :, sl]
    kh = k_ref[0, bb, :, sl] * jnp.asarray(dh ** -0.5, BF16)
    vh = jnp.concatenate([v_ref[0, bb, :, sl], jnp.ones((L, V7X_LANES), BF16)], axis=1)
    b_col = cols[:, hd:hd + 1]
    a_inter = b_col + cols[:, rows + hd:rows + hd + 1]
    dec = decay[hd:hd + 1, 0:1]
    r1_row = r1[hd:hd + 1, :]

    r1_max = jnp.max(jnp.where(causal, r1_row, -jnp.inf), axis=-1, keepdims=True)
    m_t = jnp.maximum(a_inter, b_col + r1_max)
    qk = lax.dot_general(qh, kh, (((1,), (1,)), ((), ())), preferred_element_type=F32)
    c_old = c_scr[bb, hd]
    qc = jnp.dot(qh, c_old.astype(BF16), preferred_element_type=F32)
    yield
    sc = qk * jnp.exp(jnp.where(causal, (b_col - m_t) + r1_row, -jnp.inf))
    wic = jnp.exp(a_inter - m_t)
    enc = jnp.exp(-m_t)
    both = wic * qc + jnp.dot(sc.astype(BF16), vh, preferred_element_type=F32)
    wk_t = (kh.T.astype(F32) * uexp[hd:hd + 1, :]).astype(BF16)
    c_scr[bb, hd] = dec * c_old + jnp.dot(wk_t, vh, preferred_element_type=F32)
    yield
    num = both[:, :dh]
    den = both[:, dh:dh + 1]
    hh = num * (1.0 / jnp.maximum(jnp.abs(den), enc))
    y_ref[bb, :, sl] = hh.astype(BF16)


def _mlstm(layer, segs, ifg, *, nb=4):
    _, b, s, d = segs.shape
    L = ML_CHUNK
    dh = d // ML_HEADS

    def seg_spec(slot):
        return pl.BlockSpec((1, nb, L, d), lambda bi, ci: (slot, bi, ci, 0))

    vmem = nb * (2 * (3 * L * d * 2 + IFG_ROWS * L * 4 + L * d * 2)
                 + ML_HEADS * dh * (dh + V7X_LANES) * 4 + 16 * L * d * 4)
    return pl.pallas_call(
        _mlstm_kernel,
        grid=(b // nb, s // L),
        in_specs=[
            seg_spec(SLOT_Q), seg_spec(SLOT_K), seg_spec(SLOT_V),
            pl.BlockSpec((nb, IFG_ROWS, L), lambda bi, ci: (bi, 0, ci)),
        ],
        out_specs=pl.BlockSpec((nb, L, d), lambda bi, ci: (bi, ci, 0)),
        out_shape=jax.ShapeDtypeStruct((b, s, d), BF16),
        scratch_shapes=[
            pltpu.VMEM((nb, ML_HEADS, dh, dh + V7X_LANES), F32),
            pltpu.VMEM((nb, V7X_SUBLANES, L), F32),
        ],
        compiler_params=_params(2, vmem),
        name=f"mlstm{layer}",
    )(segs, segs, segs, ifg)


def _merge_kernel(ya_ref, hm_ref, og_ref, ga_ref, gb_ref, h_ref, ng_ref, wa_ref, wb_ref, wm_ref,
                  o_ref, yb_scr, *, n_blk):
    tm, d = yb_scr.shape
    dh = d // ML_HEADS
    bm = tm // n_blk
    for blk in range(n_blk):
        rows = slice(blk * bm, (blk + 1) * bm)
        for r0 in range(blk * bm, (blk + 1) * bm, TOK_BLOCK):
            for hd in range(ML_HEADS):
                sl = slice(hd * dh, (hd + 1) * dh)
                hh = hm_ref[0, r0:r0 + TOK_BLOCK, sl].astype(F32)
                mu = jnp.mean(hh, axis=-1, keepdims=True)
                cen = hh - mu
                var = jnp.mean(cen * cen, axis=-1, keepdims=True)
                hn = (cen * lax.rsqrt(var + EPS)) * ng_ref[0][:, sl]
                gate = jax.nn.sigmoid(og_ref[0, 0, r0:r0 + TOK_BLOCK, sl].astype(F32))
                yb_scr[r0:r0 + TOK_BLOCK, sl] = (gate * hn).astype(BF16)
        pa = jnp.dot(ya_ref[0, rows, :], wa_ref[0], preferred_element_type=F32)
        pb = jnp.dot(yb_scr[rows, :], wb_ref[0], preferred_element_type=F32)
        y = (jax.nn.sigmoid(ga_ref[0, 0, rows, :].astype(F32)) * pa
             + jax.nn.sigmoid(gb_ref[0, 0, rows, :].astype(F32)) * pb)
        o_ref[0, rows, :] = h_ref[0, rows, :] + jnp.dot(y.astype(BF16), wm_ref[0],
                                                        preferred_element_type=F32)


def _merge(layer, ya, hm, segs, h, ml_norm_g, wa, wb, wm, *, tm=1024, n_blk=2):
    b, s, d = h.shape
    l = layer

    def seg_spec(slot):
        return pl.BlockSpec((1, 1, tm, d), lambda bi, i: (slot, bi, i, 0))

    w_spec = pl.BlockSpec((1, d, d), lambda bi, i: (l, 0, 0))
    row_spec = pl.BlockSpec((1, tm, d), lambda bi, i: (bi, i, 0))
    vmem = 2 * (5 * tm * d * 2 + 2 * tm * d * 4 + 3 * d * d * 2) + tm * d * 2 + 4 * tm * d * 4
    return pl.pallas_call(
        functools.partial(_merge_kernel, n_blk=n_blk),
        grid=(b, s // tm),
        in_specs=[row_spec, row_spec, seg_spec(SLOT_O), seg_spec(SLOT_GA), seg_spec(SLOT_GB),
                  row_spec, pl.BlockSpec((1, 1, d), lambda bi, i: (l, 0, 0)),
                  w_spec, w_spec, w_spec],
        out_specs=row_spec,
        out_shape=jax.ShapeDtypeStruct((b, s, d), F32),
        scratch_shapes=[pltpu.VMEM((tm, d), BF16)],
        compiler_params=_params(2, vmem),
        name=f"merge{layer}",
    )(ya, hm, segs, segs, segs, h, ml_norm_g, wa, wb, wm)


def _xattn_kernel(h_ref, g_ref, wq_ref, kv_ref, wo_ref, o_ref, q_scr, att_scr, *, n_blk):
    tm, d = att_scr.shape
    dh = d // XA_HEADS
    bm = tm // n_blk
    scale = dh ** -0.5

    def head(rows, hd):
        sl = slice(hd * dh, (hd + 1) * dh)
        kh = kv_ref[0, 0, :, sl]
        vh = kv_ref[0, 0, :, d + hd * dh:d + (hd + 1) * dh]
        sc = lax.dot_general(q_scr[rows, sl], kh, (((1,), (1,)), ((), ())),
                             preferred_element_type=F32) * scale
        top = jnp.max(sc, axis=-1, keepdims=True)
        yield
        e = jnp.exp(sc - top)
        den = jnp.sum(e, axis=-1, keepdims=True)
        yield
        p = e * (1.0 / den)
        att_scr[rows, sl] = jnp.dot(p.astype(BF16), vh, preferred_element_type=F32).astype(BF16)

    for blk in range(n_blk):
        rows = slice(blk * bm, (blk + 1) * bm)
        xn = _rms_norm(h_ref[0, rows, :], g_ref[0]).astype(BF16)
        q_scr[rows, :] = jnp.dot(xn, wq_ref[0], preferred_element_type=F32).astype(BF16)
        heads = [head(rows, hd) for hd in range(XA_HEADS)]
        while heads:
            heads = [h for h in heads if next(h, _DONE) is not _DONE]
        o_ref[0, rows, :] = h_ref[0, rows, :] + jnp.dot(att_scr[rows, :], wo_ref[0],
                                                        preferred_element_type=F32)


def _xattn(layer, h, norm_g, wq, kv, wo, *, tm=1024, n_blk=2):
    b, s, d = h.shape
    m = kv.shape[2]
    l = layer
    row_spec = pl.BlockSpec((1, tm, d), lambda bi, i: (bi, i, 0))
    w_spec = pl.BlockSpec((1, d, d), lambda bi, i: (l, 0, 0))
    vmem = 2 * (2 * tm * d * 4 + 2 * d * d * 2 + m * 2 * d * 2) + 2 * tm * d * 2 + 6 * tm * d * 4
    return pl.pallas_call(
        functools.partial(_xattn_kernel, n_blk=n_blk),
        grid=(b, s // tm),
        in_specs=[
            row_spec,
            pl.BlockSpec((1, 1, d), lambda bi, i: (l, 0, 0)),
            w_spec,
            pl.BlockSpec((1, 1, m, 2 * d), lambda bi, i: (l, bi, 0, 0)),
            w_spec,
        ],
        out_specs=row_spec,
        out_shape=jax.ShapeDtypeStruct((b, s, d), F32),
        scratch_shapes=[pltpu.VMEM((tm, d), BF16),
                        pltpu.VMEM((tm, d), BF16)],
        compiler_params=_params(2, vmem),
        name=f"xattn{layer}",
    )(h, norm_g, wq, kv, wo)


def _ffn_kernel(x_ref, g_ref, wg_ref, wu_ref, cwg_ref, cwu_ref, cbg_ref, cbu_ref, wd_ref, fg_ref,
                o_ref, xn_scr, acc_scr, pc_scr, act_scr, carry_scr, cwb_scr, *, tm, rc, n_j, n_blk,
                final_norm):
    i = pl.program_id(1)
    j = pl.program_id(2)
    halo = (FFN_CONV_W - 1) * V7X_SUBLANES
    tf = act_scr.shape[1]

    @pl.when(j == 0)
    def _():
        xn_scr[...] = _rms_norm(x_ref[0], g_ref[0]).astype(BF16)
        acc_scr[...] = jnp.zeros_like(acc_scr)

    @pl.when(i == 0)
    def _():
        carry_scr[j] = jnp.zeros(carry_scr.shape[1:], F32)

    for half in range(2):
        pc_scr[half, 0:halo, :] = carry_scr[j, half]

    for half, (cw_ref, cb_ref) in enumerate(((cwg_ref, cbg_ref), (cwu_ref, cbu_ref))):
        cwb_scr[half] = _replicate_rows(jnp.concatenate([cw_ref[0], cb_ref[0]], axis=0))

    def conv_run(half, r0, lanes):
        base = halo + r0
        return _causal_conv_run(pc_scr[half, base - halo:base, lanes],
                                pc_scr[half, base:base + TOK_BLOCK, lanes],
                                cwb_scr[half, :, :, lanes])

    bm = tm // n_blk
    d = acc_scr.shape[1]
    pw = V7X_MXU_DIM

    def up_pieces(b):
        def piece(half, w_ref, n0):
            pc_scr[half, halo + b * bm:halo + (b + 1) * bm, n0:n0 + pw] = jnp.dot(
                xn_scr[b * bm:(b + 1) * bm, :], w_ref[0, 0, :, n0:n0 + pw],
                preferred_element_type=F32)
        return [functools.partial(piece, half, w_ref, n0)
                for half, w_ref in enumerate((wg_ref, wu_ref)) for n0 in range(0, tf, pw)]

    def epilogue_items(b):
        def item(r0, lanes):
            gpre = conv_run(0, r0, lanes)
            upre = conv_run(1, r0, lanes)
            act_scr[r0:r0 + TOK_BLOCK, lanes] = (_gelu_tanh(gpre) * upre).astype(BF16)
        return [functools.partial(item, r0, slice(l0, l0 + V7X_LANES))
                for r0 in range(b * bm, (b + 1) * bm, TOK_BLOCK)
                for l0 in range(0, tf, V7X_LANES)]

    def down_pieces(b):
        rows = slice(b * bm, (b + 1) * bm)

        def piece(n0):
            acc_scr[rows, n0:n0 + pw] += jnp.dot(act_scr[rows, :], wd_ref[0, :, n0:n0 + pw],
                                                 preferred_element_type=F32)
        return [functools.partial(piece, n0) for n0 in range(0, d, pw)]

    def interleave(major, minor):
        n, k = len(minor), len(major)
        for i, piece in enumerate(major):
            piece()
            for item in minor[i * n // k:(i + 1) * n // k]:
                item()

    interleave(up_pieces(0), [])
    for b in range(n_blk):
        matmuls = (up_pieces(b + 1) if b + 1 < n_blk else []) + (down_pieces(b - 1) if b else [])
        interleave(matmuls, epilogue_items(b))
    interleave(down_pieces(n_blk - 1), [])

    for half in range(2):
        carry_scr[j, half] = pc_scr[half, tm:tm + halo, :]

    @pl.when(j == n_j - 1)
    def _():
        oc = 8 * rc

        def out_body(r0):
            res = x_ref[0, pl.ds(r0, oc), :] + acc_scr[pl.ds(r0, oc), :]
            if final_norm:
                res = _rms_norm(res, fg_ref[...])
            o_ref[0, pl.ds(r0, oc), :] = res
        _row_chunks(tm, oc, out_body)


def _ffn(layer, h, norm_g, w_up, conv_w, conv_b, w_down, final_g, *, final_norm, tm=1024, rc=32,
         n_blk=2):
    b, s, d = h.shape
    tf = w_up.shape[3]
    n_j = w_up.shape[1] // 2
    halo = (FFN_CONV_W - 1) * V7X_SUBLANES
    assert (tm // n_blk) % TOK_BLOCK == 0
    l = layer
    kern = functools.partial(_ffn_kernel, tm=tm, rc=rc, n_j=n_j, n_blk=n_blk,
                             final_norm=final_norm)
    row_spec = pl.BlockSpec((1, tm, d), lambda bi, i, j: (bi, i, 0))
    vmem = (4 * tm * d * 4 + 2 * (2 * d * tf * 2 + tf * d * 2) + tm * d * 2 + tm * d * 4
            + 2 * (tm + halo) * tf * 4 + tm * tf * 2 + n_j * 2 * halo * tf * 4 + 2 * tm * tf * 4)
    return pl.pallas_call(
        kern,
        grid=(b, s // tm, n_j),
        in_specs=[
            row_spec,
            pl.BlockSpec((1, 1, d), lambda bi, i, j: (l, 0, 0)),
            pl.BlockSpec((1, 1, d, tf), lambda bi, i, j: (l, j, 0, 0)),
            pl.BlockSpec((1, 1, d, tf), lambda bi, i, j: (l, n_j + j, 0, 0)),
            pl.BlockSpec((1, FFN_CONV_W, tf), lambda bi, i, j: (l, 0, j)),
            pl.BlockSpec((1, FFN_CONV_W, tf), lambda bi, i, j: (l, 0, n_j + j)),
            pl.BlockSpec((1, 1, tf), lambda bi, i, j: (l, 0, j)),
            pl.BlockSpec((1, 1, tf), lambda bi, i, j: (l, 0, n_j + j)),
            pl.BlockSpec((1, tf, d), lambda bi, i, j: (l, j, 0)),
            pl.BlockSpec((1, d), lambda bi, i, j: (0, 0)),
        ],
        out_specs=row_spec,
        out_shape=jax.ShapeDtypeStruct((b, s, d), F32),
        scratch_shapes=[
            pltpu.VMEM((tm, d), BF16),
            pltpu.VMEM((tm, d), F32),
            pltpu.VMEM((2, tm + halo, tf), F32),
            pltpu.VMEM((tm, tf), BF16),
            pltpu.VMEM((n_j, 2, halo, tf), F32),
            pltpu.VMEM((2, FFN_CONV_W + 1, V7X_SUBLANES, tf), F32),
        ],
        compiler_params=_params(3, vmem),
        name=f"ffn{layer}",
    )(h, norm_g, w_up, w_up, conv_w, conv_w, conv_b, conv_b, w_down, final_g)


def _block_diag_groups(w):
    depth, n_blocks, bs, _ = w.shape
    per = V7X_MXU_DIM // bs
    wg = w.reshape(depth, n_blocks // per, per, bs, bs)
    eye = jnp.eye(per, dtype=w.dtype)
    out = wg[:, :, :, :, None, :] * eye[None, None, :, None, :, None]
    return out.reshape(depth, n_blocks // per, per * bs, per * bs)


def kernel(x, mem, norm_mix_g, w_in, rnn_conv_w, rnn_conv_b, lru_wa, lru_ba, lru_wx, lru_bx, lru_lambda, ml_conv_w, ml_conv_b, ml_if_b, ml_norm_g, w_branch_a, w_branch_b, w_mix_out, norm_xa_g, xa_wq, xa_wkv, xa_wo, norm_ffn_g, ffn_w_up, ffn_conv_w, ffn_conv_b, ffn_w_down, mem_norm_g, final_norm_g):
    depth, d, _ = w_in.shape
    d_ml = ml_norm_g.shape[1]
    assert d == d_ml == lru_lambda.shape[1], "kernels assume D_RNN == D_ML == D_MODEL"
    assert ML_HEADS <= V7X_SUBLANES and x.shape[1] % ML_CHUNK == 0

    o_xr, o_gr, o_q, o_k, o_v, o_o = (n * d for n in range(6))
    o_if = 6 * d
    o_ga = o_if + 2 * ML_HEADS
    o_gb = o_ga + d

    def cols(o):
        return w_in[:, :, o:o + d]

    w_main = jnp.stack(
        [cols(o_gr), cols(o_xr), cols(o_q), cols(o_k), cols(o_v), cols(o_o), cols(o_ga), cols(o_gb)],
        axis=1).astype(BF16)
    w_if = jnp.swapaxes(w_in[:, :, o_if:o_ga], 1, 2)
    pad = jnp.zeros((depth, V7X_SUBLANES - ML_HEADS, d), w_in.dtype)
    w_ifT = jnp.concatenate([w_if[:, :ML_HEADS], pad, w_if[:, ML_HEADS:], pad], axis=1).astype(BF16)
    bpad = jnp.zeros((depth, V7X_SUBLANES - ML_HEADS), ml_if_b.dtype)
    b_if = jnp.concatenate([ml_if_b[:, :ML_HEADS], bpad, ml_if_b[:, ML_HEADS:], bpad], axis=1)[..., None]

    conv_w = jnp.stack([rnn_conv_w, ml_conv_w[:, :, :d], ml_conv_w[:, :, d:]], axis=1)
    conv_b = jnp.stack([rnn_conv_b, ml_conv_b[:, :d], ml_conv_b[:, d:]], axis=1)[:, :, None, :]
    w_gate = jnp.concatenate([_block_diag_groups(lru_wa), _block_diag_groups(lru_wx)], axis=-1).astype(BF16)
    b_gate = jnp.stack([lru_ba, lru_bx], axis=1)[:, :, None, :]

    def row(g):
        return g[:, None, :]

    wa, wb, wm = (w.astype(BF16) for w in (w_branch_a, w_branch_b, w_mix_out))
    wq, wkv, wo = (w.astype(BF16) for w in (xa_wq, xa_wkv, xa_wo))
    w_down = ffn_w_down.astype(BF16)
    w_up = ffn_w_up.reshape(depth, d, -1, FFN_TILE)
    w_up = jnp.swapaxes(w_up, 1, 2).astype(BF16)
    ffn_cb = row(ffn_conv_b)

    kv = _memkv(mem, mem_norm_g[None, :], wkv)
    h = _to_block_order(x)
    for l in range(depth):
        segs, ya, ifg = _inproj(l, h, row(norm_mix_g), w_main, w_ifT, b_if, conv_w, conv_b, w_gate,
                                b_gate, row(lru_lambda))
        hm = _mlstm(l, segs, ifg)
        h = _merge(l, ya, hm, segs, h, row(ml_norm_g), wa, wb, wm)
        h = _xattn(l, h, row(norm_xa_g), wq, kv, wo)
        h = _ffn(l, h, row(norm_ffn_g), w_up, ffn_conv_w, ffn_cb, w_down, final_norm_g[None, :],
                 final_norm=(l == depth - 1))
    return _from_block_order(h)
```

```python
import functools

import jax
import jax.numpy as jnp
from jax import lax
from jax.experimental import pallas as pl
from jax.experimental.pallas import tpu as pltpu

F32 = jnp.float32
BF16 = jnp.bfloat16

EPS = 1e-6
LRU_C = 8.0
CONV_W = 4
FFN_CONV_W = 3
RNN_BLOCK = 64
ML_HEADS = 4
ML_CHUNK = 128
XA_HEADS = 4
FFN_TILE = 1024

V7X_SUBLANES = 8
V7X_LANES = 128
V7X_MXU_DIM = 256
V7X_VMEM_BYTES = 64 * 1024 * 1024
V7X_VMEM_REQUEST_CAP = 56 * 1024 * 1024

SEG_GR, SEG_XR, SEG_Q, SEG_K, SEG_V, SEG_O, SEG_GA, SEG_GB = range(8)
SLOT_Q, SLOT_K, SLOT_V, SLOT_O, SLOT_GA, SLOT_GB = range(6)
N_SEG = 8
N_SLOT = 6
FIRST_STORED_SEG = SEG_Q
IFG_ROWS = 16


def _vmem_limit(n_bytes):
    return int(min(V7X_VMEM_REQUEST_CAP, n_bytes + n_bytes // 2 + (4 << 20)))


def _params(n_grid, vmem_bytes):
    return pltpu.CompilerParams(
        dimension_semantics=("arbitrary",) * n_grid,
        vmem_limit_bytes=_vmem_limit(vmem_bytes),
    )


def _rms_norm(xf, g):
    ms = jnp.mean(xf * xf, axis=-1, keepdims=True)
    return (xf * lax.rsqrt(ms + EPS)) * g


def _gelu_tanh(x):
    c0 = (2.0 / jnp.pi) ** 0.5
    c1 = 0.044715 * c0
    half = 0.5 * x
    return half + half * jnp.tanh(x * (c0 + c1 * (x * x)))


def _softplus(z):
    return jnp.maximum(z, 0.0) + jnp.log1p(jnp.exp(-jnp.abs(z)))


def _row_chunks(n_rows, chunk, body):
    assert n_rows % chunk == 0

    def step(r, carry):
        body(pl.multiple_of(r * chunk, chunk))
        return carry

    lax.fori_loop(0, n_rows // chunk, step, 0)


TOK_BLOCK = ML_CHUNK
TOK_SEG = TOK_BLOCK // V7X_SUBLANES


def _to_block_order(t):
    b, s, d = t.shape
    t = t.reshape(b, s // TOK_BLOCK, V7X_SUBLANES, TOK_SEG, d)
    return jnp.swapaxes(t, 2, 3).reshape(b, s, d)


def _from_block_order(t):
    b, s, d = t.shape
    t = t.reshape(b, s // TOK_BLOCK, TOK_SEG, V7X_SUBLANES, d)
    return jnp.swapaxes(t, 2, 3).reshape(b, s, d)


def _time_of_row(row):
    return (row % V7X_SUBLANES) * TOK_SEG + row // V7X_SUBLANES


def _row_of_time(tau):
    return (tau % TOK_SEG) * V7X_SUBLANES + tau // TOK_SEG


def _halo_vregs(prev_tail, cur_tail):
    sub = lax.broadcasted_iota(jnp.int32, (V7X_SUBLANES, cur_tail.shape[1]), 0)
    out = []
    for i in range(cur_tail.shape[0] // V7X_SUBLANES):
        rows = slice(i * V7X_SUBLANES, (i + 1) * V7X_SUBLANES)
        mixed = jnp.where(sub == V7X_SUBLANES - 1, prev_tail[rows], cur_tail[rows])
        out.append(pltpu.roll(mixed, 1, 0))
    return out


def _replicate_rows(rows_1w):
    k, w = rows_1w.shape
    return jnp.broadcast_to(rows_1w[:, None, :], (k, V7X_SUBLANES, w))


def _causal_conv_run(prev_tail, cur, cwb):
    taps = cwb.shape[0] - 1
    nsub = V7X_SUBLANES
    n, w = cur.shape
    tail = (taps - 1) * nsub
    ext = jnp.concatenate(_halo_vregs(prev_tail, cur[n - tail:]) + [cur], axis=0)
    ext = ext.reshape(ext.shape[0] // nsub, nsub, w)
    out = cwb[taps][None]
    for k in range(taps):
        out = out + ext[k:k + n // nsub] * cwb[k][None]
    return out.reshape(n, w)


def _memkv_kernel(mem_ref, g_ref, w_ref, o_ref):
    memn = _rms_norm(mem_ref[0], g_ref[...]).astype(BF16)
    o_ref[0, 0] = jnp.dot(memn, w_ref[0], preferred_element_type=F32).astype(BF16)


def _memkv(mem, mem_g, wkv):
    depth, d, n = wkv.shape
    b, m, _ = mem.shape
    vmem = 2 * (m * d * 4 + d * n * 2 + m * n * 2) + m * n * 4
    return pl.pallas_call(
        _memkv_kernel,
        grid=(depth, b),
        in_specs=[
            pl.BlockSpec((1, m, d), lambda l, i: (i, 0, 0)),
            pl.BlockSpec((1, d), lambda l, i: (0, 0)),
            pl.BlockSpec((1, d, n), lambda l, i: (l, 0, 0)),
        ],
        out_specs=pl.BlockSpec((1, 1, m, n), lambda l, i: (l, i, 0, 0)),
        out_shape=jax.ShapeDtypeStruct((depth, b, m, n), BF16),
        compiler_params=_params(2, vmem),
        name="memkv",
    )(mem, mem_g, wkv)


def _inproj_kernel(x_ref, g_ref, w_ref, wif_ref, bif_ref, cw_ref, cb_ref, wg_ref, bg_ref,
                   lam_ref, seg_ref, ya_ref, ifg_ref,
                   xn_scr, pc_scr, gelu_scr, xc_scr, xcb_scr, gt_scr, a_scr, u_scr,
                   carry_scr, hlast_scr, cwb_scr, *, tm, rc, n_blk, n_steps):
    m = pl.program_id(1)
    seg_done = (m + N_SEG - 1) % N_SEG
    c = xn_scr.shape[1]
    halo = (CONV_W - 1) * V7X_SUBLANES
    gw = V7X_MXU_DIM
    chunks = range(0, tm, rc)
    runs = [(r0, slice(l0, l0 + V7X_LANES))
            for r0 in range(0, tm, TOK_BLOCK) for l0 in range(0, c, V7X_LANES)]

    def norm_and_gates():
        xn = _rms_norm(x_ref[0], g_ref[0]).astype(BF16)
        xn_scr[...] = xn
        ifg = lax.dot_general(wif_ref[0], xn, (((1,), (1,)), ((), ())),
                              preferred_element_type=F32)
        ifg_ref[0] = ifg + bif_ref[0]

    bm = tm // n_blk

    pieces = [(b, n0) for b in range(n_blk) for n0 in range(0, c, gw)]

    def project_piece(par, b, n0):
        pc_scr[par, halo + b * bm:halo + (b + 1) * bm, n0:n0 + gw] = jnp.dot(
            xn_scr[b * bm:(b + 1) * bm, :], w_ref[0, :, n0:n0 + gw],
            preferred_element_type=F32)

    def project(par):
        for b, n0 in pieces:
            project_piece(par, b, n0)

    def interleave(par_next, items):
        n, k = len(items), len(pieces)
        for i, (b, n0) in enumerate(pieces):
            project_piece(par_next, b, n0)
            for item in items[i * n // k:(i + 1) * n // k]:
                item()

    def conv_run(par, ci, r0, lanes):
        base = halo + r0
        return _causal_conv_run(pc_scr[par, base - halo:base, lanes],
                                pc_scr[par, base:base + TOK_BLOCK, lanes],
                                cwb_scr[ci, :, :, lanes])

    def load_carry(par, ci):
        pc_scr[par, 0:halo, :] = carry_scr[ci]

    def save_carry(par, ci):
        carry_scr[ci] = pc_scr[par, tm:tm + halo, :]

    def gate_branch(par):
        def item(r0):
            gelu_scr[r0:r0 + rc, :] = _gelu_tanh(pc_scr[par, halo + r0:halo + r0 + rc, :])
        return [functools.partial(item, r0) for r0 in chunks]

    def lru_conv(par):
        def conv_item(r0, lanes):
            xc = conv_run(par, 0, r0, lanes)
            xc_scr[r0:r0 + TOK_BLOCK, lanes] = xc
            xcb_scr[r0:r0 + TOK_BLOCK, lanes] = xc.astype(BF16)
        return ([functools.partial(load_carry, par, 0)]
                + [functools.partial(conv_item, r0, lanes) for r0, lanes in runs]
                + [functools.partial(save_carry, par, 0)])

    def lru_gates(groups):
        state = {}

        def gate_dot(g):
            sl = slice(g * gw, (g + 1) * gw)
            gt_scr[...] = jnp.dot(xcb_scr[:, sl], wg_ref[0, g], preferred_element_type=F32)
            state["c_row"] = -LRU_C * _softplus(-lam_ref[0][:, sl])

        def gate_item(g, r0):
            sl = slice(g * gw, (g + 1) * gw)
            r = jax.nn.sigmoid(gt_scr[r0:r0 + rc, :gw] + bg_ref[0, 0][:, sl])
            ig = jax.nn.sigmoid(gt_scr[r0:r0 + rc, gw:] + bg_ref[0, 1][:, sl])
            log_a = state["c_row"] * r
            a = jnp.exp(log_a)
            mult = jnp.sqrt(-jnp.tanh(log_a) * (a * a + 1.0))
            a_scr[r0:r0 + rc, sl] = a
            u_scr[r0:r0 + rc, sl] = mult * (ig * xc_scr[r0:r0 + rc, sl])

        items = []
        for g in groups:
            items.append(functools.partial(gate_dot, g))
            items += [functools.partial(gate_item, g, r0) for r0 in chunks]
        return items

    def lru_scan():
        state = {}

        def scan_item(r0, lanes):
            nsub = V7X_SUBLANES
            sub = lax.broadcasted_iota(jnp.int32, (nsub, V7X_LANES), 0)
            key = ("h", lanes.start)
            hprev = hlast_scr[:, lanes] if r0 == 0 else state[key]
            local, decay = [], []
            for r in range(TOK_SEG):
                rows = slice(r0 + r * nsub, r0 + (r + 1) * nsub)
                a = a_scr[rows, lanes]
                u = u_scr[rows, lanes]
                local.append(u if r == 0 else a * local[-1] + u)
                decay.append(a if r == 0 else a * decay[-1])
            ea, eu = decay[-1], local[-1]
            for k in (1, 2, 4):
                keep = sub >= k
                eu = jnp.where(keep, eu + ea * pltpu.roll(eu, k, 0), eu)
                ea = jnp.where(keep, ea * pltpu.roll(ea, k, 0), ea)
            ends = eu + ea * hprev
            h_in = pltpu.roll(jnp.where(sub == nsub - 1, hprev, ends), 1, 0)
            for r in range(0, TOK_SEG, 2):
                rows = slice(r0 + r * nsub, r0 + (r + 2) * nsub)
                h = jnp.concatenate([local[r] + decay[r] * h_in,
                                     local[r + 1] + decay[r + 1] * h_in], axis=0)
                ya_ref[0, rows, lanes] = (gelu_scr[rows, lanes] * h).astype(BF16)
            hnew = jnp.broadcast_to(ends[nsub - 1:nsub, :], (nsub, V7X_LANES))
            state[key] = hnew
            if r0 + TOK_BLOCK == tm:
                hlast_scr[:, lanes] = hnew

        return [functools.partial(scan_item, r0, lanes) for r0, lanes in runs]

    def conv_silu(ci, par):
        def item(r0, lanes):
            half = 0.5 * conv_run(par, ci, r0, lanes)
            seg_ref[0, 0, r0:r0 + TOK_BLOCK, lanes] = (half + half * jnp.tanh(half)).astype(BF16)
        return ([functools.partial(load_carry, par, ci)]
                + [functools.partial(item, r0, lanes) for r0, lanes in runs]
                + [functools.partial(save_carry, par, ci)])

    def plain(par):
        def item(r0):
            seg_ref[0, 0, r0:r0 + rc, :] = pc_scr[par, halo + r0:halo + r0 + rc, :].astype(BF16)
        return [functools.partial(item, r0) for r0 in chunks]

    n_groups = c // gw
    schedule = {
        SEG_GR: lambda par: gate_branch(par),
        SEG_XR: lambda par: lru_conv(par) + lru_gates(range(0, 1)),
        SEG_Q: lambda par: conv_silu(1, par),
        SEG_K: lambda par: conv_silu(2, par),
        SEG_V: lambda par: plain(par) + lru_gates(range(1, n_groups - 1)),
        SEG_O: lambda par: plain(par) + lru_gates(range(n_groups - 1, n_groups)),
        SEG_GA: lambda par: plain(par) + lru_scan(),
        SEG_GB: lambda par: plain(par),
    }

    @pl.when(m == 0)
    def _():
        carry_scr[...] = jnp.zeros_like(carry_scr)
        hlast_scr[...] = jnp.zeros_like(hlast_scr)
        for ci in range(cwb_scr.shape[0]):
            cwb_scr[ci] = _replicate_rows(jnp.concatenate([cw_ref[0, ci], cb_ref[0, ci]], axis=0))
        norm_and_gates()
        project(0)

    for j in range(N_SEG):
        @pl.when((m > 0) & (seg_done == j))
        def _(j=j):
            par = j % 2
            items = schedule[j](par)
            if j + 1 < N_SEG:
                interleave(1 - par, items)
            else:
                for item in items:
                    item()

                @pl.when(m < n_steps)
                def _():
                    norm_and_gates()
                    project(1 - par)


def _inproj(layer, h, norm_g, w_main, w_ifT, b_if, conv_w, conv_b, w_gate, b_gate, lam, *, tm=1024,
            rc=32, n_blk=2):
    b, s, d = h.shape
    n_groups = d // V7X_MXU_DIM
    halo = (CONV_W - 1) * V7X_SUBLANES
    assert tm % TOK_BLOCK == 0
    n_tiles = s // tm
    n_steps = n_tiles * N_SEG
    kern = functools.partial(_inproj_kernel, tm=tm, rc=rc, n_blk=n_blk, n_steps=n_steps)
    vmem = (2 * tm * d * 4 + 2 * d * d * 2 + 2 * tm * d * 2 + 2 * IFG_ROWS * tm * 4
            + tm * d * 2 + 2 * (tm + halo) * d * 4 + 4 * tm * d * 4 + tm * d * 2
            + 2 * n_groups * V7X_MXU_DIM * 2 * V7X_MXU_DIM * 2 + 3 * tm * d * 4)
    l = layer

    def proj_tile(m):
        return jnp.minimum(m // N_SEG, n_tiles - 1)

    def weight_block(m):
        seg = m % N_SEG
        return jnp.where(seg <= SEG_XR, SEG_XR - seg, seg)

    def done_tile(m):
        return jnp.maximum(m - 1, 0) // N_SEG

    def done_slot(m):
        return jnp.maximum(jnp.maximum(m - 1, 0) % N_SEG - FIRST_STORED_SEG, 0)

    return pl.pallas_call(
        kern,
        grid=(b, n_steps + 1),
        in_specs=[
            pl.BlockSpec((1, tm, d), lambda bi, m: (bi, proj_tile(m), 0)),
            pl.BlockSpec((1, 1, d), lambda bi, m: (l, 0, 0)),
            pl.BlockSpec((1, d, d), lambda bi, m: (l, 0, weight_block(m))),
            pl.BlockSpec((1, IFG_ROWS, d), lambda bi, m: (l, 0, 0)),
            pl.BlockSpec((1, IFG_ROWS, 1), lambda bi, m: (l, 0, 0)),
            pl.BlockSpec((1, 3, CONV_W, d), lambda bi, m: (l, 0, 0, 0)),
            pl.BlockSpec((1, 3, 1, d), lambda bi, m: (l, 0, 0, 0)),
            pl.BlockSpec((1, n_groups, V7X_MXU_DIM, 2 * V7X_MXU_DIM), lambda bi, m: (l, 0, 0, 0)),
            pl.BlockSpec((1, 2, 1, d), lambda bi, m: (l, 0, 0, 0)),
            pl.BlockSpec((1, 1, d), lambda bi, m: (l, 0, 0)),
        ],
        out_specs=[
            pl.BlockSpec((1, 1, tm, d), lambda bi, m: (done_slot(m), bi, done_tile(m), 0)),
            pl.BlockSpec((1, tm, d), lambda bi, m: (bi, done_tile(m), 0)),
            pl.BlockSpec((1, IFG_ROWS, tm), lambda bi, m: (bi, 0, proj_tile(m))),
        ],
        out_shape=[
            jax.ShapeDtypeStruct((N_SLOT, b, s, d), BF16),
            jax.ShapeDtypeStruct((b, s, d), BF16),
            jax.ShapeDtypeStruct((b, IFG_ROWS, s), F32),
        ],
        scratch_shapes=[
            pltpu.VMEM((tm, d), BF16),
            pltpu.VMEM((2, tm + halo, d), F32),
            pltpu.VMEM((tm, d), F32),
            pltpu.VMEM((tm, d), F32),
            pltpu.VMEM((tm, d), BF16),
            pltpu.VMEM((tm, 2 * V7X_MXU_DIM), F32),
            pltpu.VMEM((tm, d), F32),
            pltpu.VMEM((tm, d), F32),
            pltpu.VMEM((3, halo, d), F32),
            pltpu.VMEM((V7X_SUBLANES, d), F32),
            pltpu.VMEM((3, CONV_W + 1, V7X_SUBLANES, d), F32),
        ],
        compiler_params=_params(2, vmem),
        name=f"inproj{layer}",
    )(h, norm_g, w_main, w_ifT, b_if, conv_w, conv_b, w_gate, b_gate, lam)


def _lane_scan(x, lane, op, fill):
    n = x.shape[-1]
    k = 1
    while k < n:
        x = op(x, jnp.where(lane >= k, pltpu.roll(x, k, 1), fill))
        k *= 2
    return x


_DONE = object()


def _mlstm_kernel(q_ref, k_ref, v_ref, ifg_ref, y_ref, c_scr, m_scr):
    @pl.when(pl.program_id(1) == 0)
    def _():
        c_scr[...] = jnp.zeros_like(c_scr)
        m_scr[...] = jnp.zeros_like(m_scr)

    nb = y_ref.shape[0]
    gates = [_mlstm_gates(bb, ifg_ref, m_scr) for bb in range(nb)]
    heads = [_mlstm_head(bb, hd, gates[bb], q_ref, k_ref, v_ref, y_ref, c_scr)
             for hd in range(ML_HEADS) for bb in range(nb)]
    while heads:
        heads = [h for h in heads if next(h, _DONE) is not _DONE]


def _mlstm_gates(bb, ifg_ref, m_scr):
    L = ML_CHUNK
    rows = V7X_SUBLANES

    ifg = ifg_ref[bb]
    li = ifg[0:rows]
    lf = -_softplus(-ifg[rows:2 * rows])
    src = lax.broadcasted_iota(jnp.int32, (L, L), 0)
    dst = lax.broadcasted_iota(jnp.int32, (L, L), 1)
    not_after = (_time_of_row(src) <= _time_of_row(dst)).astype(F32)
    bcum = jnp.dot(lf, not_after, precision=lax.Precision.HIGHEST,
                   preferred_element_type=F32)
    m_st = m_scr[bb]
    g_tot = jnp.broadcast_to(bcum[:, L - 1:L], (rows, L))
    r1 = li - bcum
    u = g_tot - bcum + li
    m_next = jnp.maximum(g_tot + m_st, jnp.max(u, axis=-1, keepdims=True))
    decay = jnp.exp(g_tot + m_st - m_next)
    uexp = jnp.exp(u - m_next)
    m_scr[bb] = m_next

    stacked = jnp.concatenate(
        [bcum, m_st, jnp.zeros((L - 2 * rows, L), F32)], axis=0)
    cols = stacked.T
    return cols, r1, decay, uexp


def _mlstm_head(bb, hd, gates, q_ref, k_ref, v_ref, y_ref, c_scr):
    L = ML_CHUNK
    dh = c_scr.shape[2]
    rows = V7X_SUBLANES
    cols, r1, decay, uexp = gates
    tt = lax.broadcasted_iota(jnp.int32, (L, L), 0)
    ss = lax.broadcasted_iota(jnp.int32, (L, L), 1)
    causal = _time_of_row(ss) <= _time_of_row(tt)

    sl = slice(hd * dh, (hd + 1) * dh)
    qh = q_ref[0, bb, :, sl]
    kh = k_ref[0, bb, :, sl] * jnp.asarray(dh ** -0.5, BF16)
    vh = jnp.concatenate([v_ref[0, bb, :, sl], jnp.ones((L, V7X_LANES), BF16)], axis=1)
    b_col = cols[:, hd:hd + 1]
    a_inter = b_col + cols[:, rows + hd:rows + hd + 1]
    dec = decay[hd:hd + 1, 0:1]
    r1_row = r1[hd:hd + 1, :]

    r1_max = jnp.max(jnp.where(causal, r1_row, -jnp.inf), axis=-1, keepdims=True)
    m_t = jnp.maximum(a_inter, b_col + r1_max)
    qk = lax.dot_general(qh, kh, (((1,), (1,)), ((), ())), preferred_element_type=F32)
    c_old = c_scr[bb, hd]
    qc = jnp.dot(qh, c_old.astype(BF16), preferred_element_type=F32)
    yield
    sc = qk * jnp.exp(jnp.where(causal, (b_col - m_t) + r1_row, -jnp.inf))
    wic = jnp.exp(a_inter - m_t)
    enc = jnp.exp(-m_t)
    both = wic * qc + jnp.dot(sc.astype(BF16), vh, preferred_element_type=F32)
    wk_t = (kh.T.astype(F32) * uexp[hd:hd + 1, :]).astype(BF16)
    c_scr[bb, hd] = dec * c_old + jnp.dot(wk_t, vh, preferred_element_type=F32)
    yield
    num = both[:, :dh]
    den = both[:, dh:dh + 1]
    hh = num * (1.0 / jnp.maximum(jnp.abs(den), enc))
    y_ref[bb, :, sl] = hh.astype(BF16)


def _mlstm(layer, segs, ifg, *, nb=4):
    _, b, s, d = segs.shape
    L = ML_CHUNK
    dh = d // ML_HEADS

    def seg_spec(slot):
        return pl.BlockSpec((1, nb, L, d), lambda bi, ci: (slot, bi, ci, 0))

    vmem = nb * (2 * (3 * L * d * 2 + IFG_ROWS * L * 4 + L * d * 2)
                 + ML_HEADS * dh * (dh + V7X_LANES) * 4 + 16 * L * d * 4)
    return pl.pallas_call(
        _mlstm_kernel,
        grid=(b // nb, s // L),
        in_specs=[
            seg_spec(SLOT_Q), seg_spec(SLOT_K), seg_spec(SLOT_V),
            pl.BlockSpec((nb, IFG_ROWS, L), lambda bi, ci: (bi, 0, ci)),
        ],
        out_specs=pl.BlockSpec((nb, L, d), lambda bi, ci: (bi, ci, 0)),
        out_shape=jax.ShapeDtypeStruct((b, s, d), BF16),
        scratch_shapes=[
            pltpu.VMEM((nb, ML_HEADS, dh, dh + V7X_LANES), F32),
            pltpu.VMEM((nb, V7X_SUBLANES, L), F32),
        ],
        compiler_params=_params(2, vmem),
        name=f"mlstm{layer}",
    )(segs, segs, segs, ifg)


def _merge_kernel(ya_ref, hm_ref, og_ref, ga_ref, gb_ref, h_ref, ng_ref, wa_ref, wb_ref, wm_ref,
                  o_ref, yb_scr, *, n_blk):
    tm, d = yb_scr.shape
    dh = d // ML_HEADS
    bm = tm // n_blk
    for blk in range(n_blk):
        rows = slice(blk * bm, (blk + 1) * bm)
        for r0 in range(blk * bm, (blk + 1) * bm, TOK_BLOCK):
            for hd in range(ML_HEADS):
                sl = slice(hd * dh, (hd + 1) * dh)
                hh = hm_ref[0, r0:r0 + TOK_BLOCK, sl].astype(F32)
                mu = jnp.mean(hh, axis=-1, keepdims=True)
                cen = hh - mu
                var = jnp.mean(cen * cen, axis=-1, keepdims=True)
                hn = (cen * lax.rsqrt(var + EPS)) * ng_ref[0][:, sl]
                gate = jax.nn.sigmoid(og_ref[0, 0, r0:r0 + TOK_BLOCK, sl].astype(F32))
                yb_scr[r0:r0 + TOK_BLOCK, sl] = (gate * hn).astype(BF16)
        pa = jnp.dot(ya_ref[0, rows, :], wa_ref[0], preferred_element_type=F32)
        pb = jnp.dot(yb_scr[rows, :], wb_ref[0], preferred_element_type=F32)
        y = (jax.nn.sigmoid(ga_ref[0, 0, rows, :].astype(F32)) * pa
             + jax.nn.sigmoid(gb_ref[0, 0, rows, :].astype(F32)) * pb)
        o_ref[0, rows, :] = h_ref[0, rows, :] + jnp.dot(y.astype(BF16), wm_ref[0],
                                                        preferred_element_type=F32)


def _merge(layer, ya, hm, segs, h, ml_norm_g, wa, wb, wm, *, tm=1024, n_blk=2):
    b, s, d = h.shape
    l = layer

    def seg_spec(slot):
        return pl.BlockSpec((1, 1, tm, d), lambda bi, i: (slot, bi, i, 0))

    w_spec = pl.BlockSpec((1, d, d), lambda bi, i: (l, 0, 0))
    row_spec = pl.BlockSpec((1, tm, d), lambda bi, i: (bi, i, 0))
    vmem = 2 * (5 * tm * d * 2 + 2 * tm * d * 4 + 3 * d * d * 2) + tm * d * 2 + 4 * tm * d * 4
    return pl.pallas_call(
        functools.partial(_merge_kernel, n_blk=n_blk),
        grid=(b, s // tm),
        in_specs=[row_spec, row_spec, seg_spec(SLOT_O), seg_spec(SLOT_GA), seg_spec(SLOT_GB),
                  row_spec, pl.BlockSpec((1, 1, d), lambda bi, i: (l, 0, 0)),
                  w_spec, w_spec, w_spec],
        out_specs=row_spec,
        out_shape=jax.ShapeDtypeStruct((b, s, d), F32),
        scratch_shapes=[pltpu.VMEM((tm, d), BF16)],
        compiler_params=_params(2, vmem),
        name=f"merge{layer}",
    )(ya, hm, segs, segs, segs, h, ml_norm_g, wa, wb, wm)


def _xattn_kernel(h_ref, g_ref, wq_ref, kv_ref, wo_ref, o_ref, q_scr, att_scr, *, n_blk):
    tm, d = att_scr.shape
    dh = d // XA_HEADS
    bm = tm // n_blk
    scale = dh ** -0.5

    def head(rows, hd):
        sl = slice(hd * dh, (hd + 1) * dh)
        kh = kv_ref[0, 0, :, sl]
        vh = kv_ref[0, 0, :, d + hd * dh:d + (hd + 1) * dh]
        sc = lax.dot_general(q_scr[rows, sl], kh, (((1,), (1,)), ((), ())),
                             preferred_element_type=F32) * scale
        top = jnp.max(sc, axis=-1, keepdims=True)
        yield
        e = jnp.exp(sc - top)
        den = jnp.sum(e, axis=-1, keepdims=True)
        yield
        p = e * (1.0 / den)
        att_scr[rows, sl] = jnp.dot(p.astype(BF16), vh, preferred_element_type=F32).astype(BF16)

    for blk in range(n_blk):
        rows = slice(blk * bm, (blk + 1) * bm)
        xn = _rms_norm(h_ref[0, rows, :], g_ref[0]).astype(BF16)
        q_scr[rows, :] = jnp.dot(xn, wq_ref[0], preferred_element_type=F32).astype(BF16)
        heads = [head(rows, hd) for hd in range(XA_HEADS)]
        while heads:
            heads = [h for h in heads if next(h, _DONE) is not _DONE]
        o_ref[0, rows, :] = h_ref[0, rows, :] + jnp.dot(att_scr[rows, :], wo_ref[0],
                                                        preferred_element_type=F32)


def _xattn(layer, h, norm_g, wq, kv, wo, *, tm=1024, n_blk=2):
    b, s, d = h.shape
    m = kv.shape[2]
    l = layer
    row_spec = pl.BlockSpec((1, tm, d), lambda bi, i: (bi, i, 0))
    w_spec = pl.BlockSpec((1, d, d), lambda bi, i: (l, 0, 0))
    vmem = 2 * (2 * tm * d * 4 + 2 * d * d * 2 + m * 2 * d * 2) + 2 * tm * d * 2 + 6 * tm * d * 4
    return pl.pallas_call(
        functools.partial(_xattn_kernel, n_blk=n_blk),
        grid=(b, s // tm),
        in_specs=[
            row_spec,
            pl.BlockSpec((1, 1, d), lambda bi, i: (l, 0, 0)),
            w_spec,
            pl.BlockSpec((1, 1, m, 2 * d), lambda bi, i: (l, bi, 0, 0)),
            w_spec,
        ],
        out_specs=row_spec,
        out_shape=jax.ShapeDtypeStruct((b, s, d), F32),
        scratch_shapes=[pltpu.VMEM((tm, d), BF16),
                        pltpu.VMEM((tm, d), BF16)],
        compiler_params=_params(2, vmem),
        name=f"xattn{layer}",
    )(h, norm_g, wq, kv, wo)


def _ffn_kernel(x_ref, g_ref, wg_ref, wu_ref, cwg_ref, cwu_ref, cbg_ref, cbu_ref, wd_ref, fg_ref,
                o_ref, xn_scr, acc_scr, pc_scr, act_scr, carry_scr, cwb_scr, *, tm, rc, n_j, n_blk,
                final_norm):
    i = pl.program_id(1)
    j = pl.program_id(2)
    halo = (FFN_CONV_W - 1) * V7X_SUBLANES
    tf = act_scr.shape[1]

    @pl.when(j == 0)
    def _():
        xn_scr[...] = _rms_norm(x_ref[0], g_ref[0]).astype(BF16)
        acc_scr[...] = jnp.zeros_like(acc_scr)

    @pl.when(i == 0)
    def _():
        carry_scr[j] = jnp.zeros(carry_scr.shape[1:], F32)

    for half in range(2):
        pc_scr[half, 0:halo, :] = carry_scr[j, half]

    for half, (cw_ref, cb_ref) in enumerate(((cwg_ref, cbg_ref), (cwu_ref, cbu_ref))):
        cwb_scr[half] = _replicate_rows(jnp.concatenate([cw_ref[0], cb_ref[0]], axis=0))

    def conv_run(half, r0, lanes):
        base = halo + r0
        return _causal_conv_run(pc_scr[half, base - halo:base, lanes],
                                pc_scr[half, base:base + TOK_BLOCK, lanes],
                                cwb_scr[half, :, :, lanes])

    bm = tm // n_blk
    d = acc_scr.shape[1]
    pw = V7X_MXU_DIM

    def up_pieces(b):
        def piece(half, w_ref, n0):
            pc_scr[half, halo + b * bm:halo + (b + 1) * bm, n0:n0 + pw] = jnp.dot(
                xn_scr[b * bm:(b + 1) * bm, :], w_ref[0, :, n0:n0 + pw],
                preferred_element_type=F32)
        return [functools.partial(piece, half, w_ref, n0)
                for half, w_ref in enumerate((wg_ref, wu_ref)) for n0 in range(0, tf, pw)]

    def epilogue_items(b):
        def item(r0, lanes):
            gpre = conv_run(0, r0, lanes)
            upre = conv_run(1, r0, lanes)
            act_scr[r0:r0 + TOK_BLOCK, lanes] = (_gelu_tanh(gpre) * upre).astype(BF16)
        return [functools.partial(item, r0, slice(l0, l0 + V7X_LANES))
                for r0 in range(b * bm, (b + 1) * bm, TOK_BLOCK)
                for l0 in range(0, tf, V7X_LANES)]

    def down_pieces(b):
        rows = slice(b * bm, (b + 1) * bm)

        def piece(n0):
            acc_scr[rows, n0:n0 + pw] += jnp.dot(act_scr[rows, :], wd_ref[0, :, n0:n0 + pw],
                                                 preferred_element_type=F32)
        return [functools.partial(piece, n0) for n0 in range(0, d, pw)]

    def interleave(major, minor):
        n, k = len(minor), len(major)
        for i, piece in enumerate(major):
            piece()
            for item in minor[i * n // k:(i + 1) * n // k]:
                item()

    interleave(up_pieces(0), [])
    for b in range(n_blk):
        matmuls = (up_pieces(b + 1) if b + 1 < n_blk else []) + (down_pieces(b - 1) if b else [])
        interleave(matmuls, epilogue_items(b))
    interleave(down_pieces(n_blk - 1), [])

    for half in range(2):
        carry_scr[j, half] = pc_scr[half, tm:tm + halo, :]

    @pl.when(j == n_j - 1)
    def _():
        oc = 8 * rc

        def out_body(r0):
            res = x_ref[0, pl.ds(r0, oc), :] + acc_scr[pl.ds(r0, oc), :]
            if final_norm:
                res = _rms_norm(res, fg_ref[...])
            o_ref[0, pl.ds(r0, oc), :] = res
        _row_chunks(tm, oc, out_body)


def _ffn(layer, h, norm_g, w_up, conv_w, conv_b, w_down, final_g, *, final_norm, tm=1024, rc=32,
         n_blk=2):
    b, s, d = h.shape
    tf = FFN_TILE
    n_j = w_up.shape[2] // (2 * tf)
    halo = (FFN_CONV_W - 1) * V7X_SUBLANES
    assert (tm // n_blk) % TOK_BLOCK == 0
    l = layer
    kern = functools.partial(_ffn_kernel, tm=tm, rc=rc, n_j=n_j, n_blk=n_blk,
                             final_norm=final_norm)
    row_spec = pl.BlockSpec((1, tm, d), lambda bi, i, j: (bi, i, 0))
    vmem = (4 * tm * d * 4 + 2 * (2 * d * tf * 2 + tf * d * 2) + tm * d * 2 + tm * d * 4
            + 2 * (tm + halo) * tf * 4 + tm * tf * 2 + n_j * 2 * halo * tf * 4 + 2 * tm * tf * 4)
    return pl.pallas_call(
        kern,
        grid=(b, s // tm, n_j),
        in_specs=[
            row_spec,
            pl.BlockSpec((1, 1, d), lambda bi, i, j: (l, 0, 0)),
            pl.BlockSpec((1, d, tf), lambda bi, i, j: (l, 0, j)),
            pl.BlockSpec((1, d, tf), lambda bi, i, j: (l, 0, n_j + j)),
            pl.BlockSpec((1, FFN_CONV_W, tf), lambda bi, i, j: (l, 0, j)),
            pl.BlockSpec((1, FFN_CONV_W, tf), lambda bi, i, j: (l, 0, n_j + j)),
            pl.BlockSpec((1, 1, tf), lambda bi, i, j: (l, 0, j)),
            pl.BlockSpec((1, 1, tf), lambda bi, i, j: (l, 0, n_j + j)),
            pl.BlockSpec((1, tf, d), lambda bi, i, j: (l, j, 0)),
            pl.BlockSpec((1, d), lambda bi, i, j: (0, 0)),
        ],
        out_specs=row_spec,
        out_shape=jax.ShapeDtypeStruct((b, s, d), F32),
        scratch_shapes=[
            pltpu.VMEM((tm, d), BF16),
            pltpu.VMEM((tm, d), F32),
            pltpu.VMEM((2, tm + halo, tf), F32),
            pltpu.VMEM((tm, tf), BF16),
            pltpu.VMEM((n_j, 2, halo, tf), F32),
            pltpu.VMEM((2, FFN_CONV_W + 1, V7X_SUBLANES, tf), F32),
        ],
        compiler_params=_params(3, vmem),
        name=f"ffn{layer}",
    )(h, norm_g, w_up, w_up, conv_w, conv_w, conv_b, conv_b, w_down, final_g)


def _block_diag_groups(w):
    depth, n_blocks, bs, _ = w.shape
    per = V7X_MXU_DIM // bs
    wg = w.reshape(depth, n_blocks // per, per, bs, bs)
    eye = jnp.eye(per, dtype=w.dtype)
    out = wg[:, :, :, :, None, :] * eye[None, None, :, None, :, None]
    return out.reshape(depth, n_blocks // per, per * bs, per * bs)


def kernel(x, mem, norm_mix_g, w_in, rnn_conv_w, rnn_conv_b, lru_wa, lru_ba, lru_wx, lru_bx, lru_lambda, ml_conv_w, ml_conv_b, ml_if_b, ml_norm_g, w_branch_a, w_branch_b, w_mix_out, norm_xa_g, xa_wq, xa_wkv, xa_wo, norm_ffn_g, ffn_w_up, ffn_conv_w, ffn_conv_b, ffn_w_down, mem_norm_g, final_norm_g):
    depth, d, _ = w_in.shape
    d_ml = ml_norm_g.shape[1]
    assert d == d_ml == lru_lambda.shape[1], "kernels assume D_RNN == D_ML == D_MODEL"
    assert ML_HEADS <= V7X_SUBLANES and x.shape[1] % ML_CHUNK == 0

    o_if = 6 * d
    o_ga = o_if + 2 * ML_HEADS

    w_main = jnp.concatenate([w_in[:, :, :o_if], w_in[:, :, o_ga:]], axis=-1).astype(BF16)
    w_if = jnp.swapaxes(w_in[:, :, o_if:o_ga], 1, 2)
    pad = jnp.zeros((depth, V7X_SUBLANES - ML_HEADS, d), w_in.dtype)
    w_ifT = jnp.concatenate([w_if[:, :ML_HEADS], pad, w_if[:, ML_HEADS:], pad], axis=1).astype(BF16)
    bpad = jnp.zeros((depth, V7X_SUBLANES - ML_HEADS), ml_if_b.dtype)
    b_if = jnp.concatenate([ml_if_b[:, :ML_HEADS], bpad, ml_if_b[:, ML_HEADS:], bpad], axis=1)[..., None]

    conv_w = jnp.stack([rnn_conv_w, ml_conv_w[:, :, :d], ml_conv_w[:, :, d:]], axis=1)
    conv_b = jnp.stack([rnn_conv_b, ml_conv_b[:, :d], ml_conv_b[:, d:]], axis=1)[:, :, None, :]
    w_gate = jnp.concatenate([_block_diag_groups(lru_wa), _block_diag_groups(lru_wx)], axis=-1).astype(BF16)
    b_gate = jnp.stack([lru_ba, lru_bx], axis=1)[:, :, None, :]

    def row(g):
        return g[:, None, :]

    wa, wb, wm = (w.astype(BF16) for w in (w_branch_a, w_branch_b, w_mix_out))
    wq, wkv, wo = (w.astype(BF16) for w in (xa_wq, xa_wkv, xa_wo))
    w_down = ffn_w_down.astype(BF16)
    w_up = ffn_w_up.astype(BF16)
    ffn_cb = row(ffn_conv_b)

    kv = _memkv(mem, mem_norm_g[None, :], wkv)
    h = _to_block_order(x)
    for l in range(depth):
        segs, ya, ifg = _inproj(l, h, row(norm_mix_g), w_main, w_ifT, b_if, conv_w, conv_b, w_gate,
                                b_gate, row(lru_lambda))
        hm = _mlstm(l, segs, ifg)
        h = _merge(l, ya, hm, segs, h, row(ml_norm_g), wa, wb, wm)
        h = _xattn(l, h, row(norm_xa_g), wq, kv, wo)
        h = _ffn(l, h, row(norm_ffn_g), w_up, ffn_conv_w, ffn_cb, w_down, final_norm_g[None, :],
                 final_norm=(l == depth - 1))
    return _from_block_order(h)
```

```python
import functools

import jax
import jax.numpy as jnp
from jax import lax
from jax.experimental import pallas as pl
from jax.experimental.pallas import tpu as pltpu

F32 = jnp.float32
BF16 = jnp.bfloat16

EPS = 1e-6
LRU_C = 8.0
CONV_W = 4
FFN_CONV_W = 3
RNN_BLOCK = 64
ML_HEADS = 4
ML_CHUNK = 128
XA_HEADS = 4
FFN_TILE = 1024

V7X_SUBLANES = 8
V7X_LANES = 128
V7X_MXU_DIM = 256
V7X_VMEM_BYTES = 64 * 1024 * 1024
V7X_VMEM_REQUEST_CAP = 56 * 1024 * 1024

SEG_GR, SEG_XR, SEG_Q, SEG_K, SEG_V, SEG_O, SEG_GA, SEG_GB = range(8)
SLOT_Q, SLOT_K, SLOT_V, SLOT_O, SLOT_GA, SLOT_GB = range(6)
N_SEG = 8
N_SLOT = 6
FIRST_STORED_SEG = SEG_Q
IFG_ROWS = 16


def _vmem_limit(n_bytes):
    return int(min(V7X_VMEM_REQUEST_CAP, n_bytes + n_bytes // 2 + (4 << 20)))


def _params(n_grid, vmem_bytes):
    return pltpu.CompilerParams(
        dimension_semantics=("arbitrary",) * n_grid,
        vmem_limit_bytes=_vmem_limit(vmem_bytes),
    )


def _rms_norm(xf, g):
    ms = jnp.mean(xf * xf, axis=-1, keepdims=True)
    return (xf * lax.rsqrt(ms + EPS)) * g


def _gelu_tanh(x):
    c0 = (2.0 / jnp.pi) ** 0.5
    c1 = 0.044715 * c0
    half = 0.5 * x
    return half + half * jnp.tanh(x * (c0 + c1 * (x * x)))


def _softplus(z):
    return jnp.maximum(z, 0.0) + jnp.log1p(jnp.exp(-jnp.abs(z)))


def _row_chunks(n_rows, chunk, body):
    assert n_rows % chunk == 0

    def step(r, carry):
        body(pl.multiple_of(r * chunk, chunk))
        return carry

    lax.fori_loop(0, n_rows // chunk, step, 0)


TOK_BLOCK = ML_CHUNK
TOK_SEG = TOK_BLOCK // V7X_SUBLANES


def _to_block_order(t):
    b, s, d = t.shape
    t = t.reshape(b, s // TOK_BLOCK, V7X_SUBLANES, TOK_SEG, d)
    return jnp.swapaxes(t, 2, 3).reshape(b, s, d)


def _from_block_order(t):
    b, s, d = t.shape
    t = t.reshape(b, s // TOK_BLOCK, TOK_SEG, V7X_SUBLANES, d)
    return jnp.swapaxes(t, 2, 3).reshape(b, s, d)


def _time_of_row(row):
    return (row % V7X_SUBLANES) * TOK_SEG + row // V7X_SUBLANES


def _row_of_time(tau):
    return (tau % TOK_SEG) * V7X_SUBLANES + tau // TOK_SEG


def _halo_vregs(prev_tail, cur_tail):
    sub = lax.broadcasted_iota(jnp.int32, (V7X_SUBLANES, cur_tail.shape[1]), 0)
    out = []
    for i in range(cur_tail.shape[0] // V7X_SUBLANES):
        rows = slice(i * V7X_SUBLANES, (i + 1) * V7X_SUBLANES)
        mixed = jnp.where(sub == V7X_SUBLANES - 1, prev_tail[rows], cur_tail[rows])
        out.append(pltpu.roll(mixed, 1, 0))
    return out


def _replicate_rows(rows_1w):
    k, w = rows_1w.shape
    return jnp.broadcast_to(rows_1w[:, None, :], (k, V7X_SUBLANES, w))


def _causal_conv_run(prev_tail, cur, cwb):
    taps = cwb.shape[0] - 1
    nsub = V7X_SUBLANES
    n, w = cur.shape
    tail = (taps - 1) * nsub
    ext = jnp.concatenate(_halo_vregs(prev_tail, cur[n - tail:]) + [cur], axis=0)
    ext = ext.reshape(ext.shape[0] // nsub, nsub, w)
    out = cwb[taps][None]
    for k in range(taps):
        out = out + ext[k:k + n // nsub] * cwb[k][None]
    return out.reshape(n, w)


def _memkv_kernel(mem_ref, g_ref, w_ref, o_ref):
    memn = _rms_norm(mem_ref[0], g_ref[...]).astype(BF16)
    o_ref[0, 0] = jnp.dot(memn, w_ref[0], preferred_element_type=F32).astype(BF16)


def _memkv(mem, mem_g, wkv):
    depth, d, n = wkv.shape
    b, m, _ = mem.shape
    vmem = 2 * (m * d * 4 + d * n * 2 + m * n * 2) + m * n * 4
    return pl.pallas_call(
        _memkv_kernel,
        grid=(depth, b),
        in_specs=[
            pl.BlockSpec((1, m, d), lambda l, i: (i, 0, 0)),
            pl.BlockSpec((1, d), lambda l, i: (0, 0)),
            pl.BlockSpec((1, d, n), lambda l, i: (l, 0, 0)),
        ],
        out_specs=pl.BlockSpec((1, 1, m, n), lambda l, i: (l, i, 0, 0)),
        out_shape=jax.ShapeDtypeStruct((depth, b, m, n), BF16),
        compiler_params=_params(2, vmem),
        name="memkv",
    )(mem, mem_g, wkv)


def _inproj_kernel(x_ref, g_ref, w_ref, wif_ref, bif_ref, cw_ref, cb_ref, wg_ref, bg_ref,
                   lam_ref, seg_ref, ya_ref, ifg_ref,
                   xn_scr, pc_scr, gelu_scr, xc_scr, xcb_scr, gt_scr, a_scr, u_scr,
                   carry_scr, hlast_scr, cwb_scr, *, tm, rc, n_blk, n_steps):
    m = pl.program_id(1)
    seg_done = (m + N_SEG - 1) % N_SEG
    c = xn_scr.shape[1]
    halo = (CONV_W - 1) * V7X_SUBLANES
    gw = V7X_MXU_DIM
    chunks = range(0, tm, rc)
    run_w = V7X_LANES
    runs = [(r0, slice(l0, l0 + run_w))
            for r0 in range(0, tm, TOK_BLOCK) for l0 in range(0, c, run_w)]

    def norm_and_gates():
        xn = _rms_norm(x_ref[0], g_ref[0]).astype(BF16)
        xn_scr[...] = xn
        ifg = lax.dot_general(wif_ref[0], xn, (((1,), (1,)), ((), ())),
                              preferred_element_type=F32)
        ifg_ref[0] = ifg + bif_ref[0]

    bm = tm // n_blk

    pieces = [(b, n0) for b in range(n_blk) for n0 in range(0, c, gw)]

    def project_piece(par, b, n0):
        pc_scr[par, halo + b * bm:halo + (b + 1) * bm, n0:n0 + gw] = jnp.dot(
            xn_scr[b * bm:(b + 1) * bm, :], w_ref[0, :, n0:n0 + gw],
            preferred_element_type=F32)

    def project(par):
        for b, n0 in pieces:
            project_piece(par, b, n0)

    def interleave(par_next, items):
        n, k = len(items), len(pieces)
        for i, (b, n0) in enumerate(pieces):
            project_piece(par_next, b, n0)
            for item in items[i * n // k:(i + 1) * n // k]:
                item()

    def conv_run(par, ci, r0, lanes):
        base = halo + r0
        return _causal_conv_run(pc_scr[par, base - halo:base, lanes],
                                pc_scr[par, base:base + TOK_BLOCK, lanes],
                                cwb_scr[ci, :, :, lanes])

    def load_carry(par, ci):
        pc_scr[par, 0:halo, :] = carry_scr[ci]

    def save_carry(par, ci):
        carry_scr[ci] = pc_scr[par, tm:tm + halo, :]

    def gate_branch(par):
        def item(r0):
            gelu_scr[r0:r0 + rc, :] = _gelu_tanh(pc_scr[par, halo + r0:halo + r0 + rc, :])
        return [functools.partial(item, r0) for r0 in chunks]

    def lru_conv(par):
        def conv_item(r0, lanes):
            xc = conv_run(par, 0, r0, lanes)
            xc_scr[r0:r0 + TOK_BLOCK, lanes] = xc
            xcb_scr[r0:r0 + TOK_BLOCK, lanes] = xc.astype(BF16)
        return ([functools.partial(load_carry, par, 0)]
                + [functools.partial(conv_item, r0, lanes) for r0, lanes in runs]
                + [functools.partial(save_carry, par, 0)])

    def lru_gates(groups):
        state = {}

        def gate_dot(g):
            sl = slice(g * gw, (g + 1) * gw)
            gt_scr[...] = jnp.dot(xcb_scr[:, sl], wg_ref[0, g], preferred_element_type=F32)
            state["c_row"] = -LRU_C * _softplus(-lam_ref[0][:, sl])

        def gate_item(g, r0):
            sl = slice(g * gw, (g + 1) * gw)
            r = jax.nn.sigmoid(gt_scr[r0:r0 + rc, :gw] + bg_ref[0, 0][:, sl])
            ig = jax.nn.sigmoid(gt_scr[r0:r0 + rc, gw:] + bg_ref[0, 1][:, sl])
            log_a = state["c_row"] * r
            a = jnp.exp(log_a)
            mult = jnp.sqrt(-jnp.tanh(log_a) * (a * a + 1.0))
            a_scr[r0:r0 + rc, sl] = a
            u_scr[r0:r0 + rc, sl] = mult * (ig * xc_scr[r0:r0 + rc, sl])

        items = []
        for g in groups:
            items.append(functools.partial(gate_dot, g))
            items += [functools.partial(gate_item, g, r0) for r0 in chunks]
        return items

    def lru_scan():
        state = {}

        def scan_item(r0, lanes):
            nsub = V7X_SUBLANES
            sub = lax.broadcasted_iota(jnp.int32, (nsub, run_w), 0)
            key = ("h", lanes.start)
            hprev = hlast_scr[:, lanes] if r0 == 0 else state[key]
            local, decay = [], []
            for r in range(TOK_SEG):
                rows = slice(r0 + r * nsub, r0 + (r + 1) * nsub)
                a = a_scr[rows, lanes]
                u = u_scr[rows, lanes]
                local.append(u if r == 0 else a * local[-1] + u)
                decay.append(a if r == 0 else a * decay[-1])
            ea, eu = decay[-1], local[-1]
            for k in (1, 2, 4):
                keep = sub >= k
                eu = jnp.where(keep, eu + ea * pltpu.roll(eu, k, 0), eu)
                ea = jnp.where(keep, ea * pltpu.roll(ea, k, 0), ea)
            ends = eu + ea * hprev
            h_in = pltpu.roll(jnp.where(sub == nsub - 1, hprev, ends), 1, 0)
            for r in range(0, TOK_SEG, 2):
                rows = slice(r0 + r * nsub, r0 + (r + 2) * nsub)
                h = jnp.concatenate([local[r] + decay[r] * h_in,
                                     local[r + 1] + decay[r + 1] * h_in], axis=0)
                ya_ref[0, rows, lanes] = (gelu_scr[rows, lanes] * h).astype(BF16)
            hnew = jnp.broadcast_to(ends[nsub - 1:nsub, :], (nsub, run_w))
            state[key] = hnew
            if r0 + TOK_BLOCK == tm:
                hlast_scr[:, lanes] = hnew

        return [functools.partial(scan_item, r0, lanes) for r0, lanes in runs]

    def conv_silu(ci, par):
        def item(r0, lanes):
            half = 0.5 * conv_run(par, ci, r0, lanes)
            seg_ref[0, 0, r0:r0 + TOK_BLOCK, lanes] = (half + half * jnp.tanh(half)).astype(BF16)
        return ([functools.partial(load_carry, par, ci)]
                + [functools.partial(item, r0, lanes) for r0, lanes in runs]
                + [functools.partial(save_carry, par, ci)])

    def plain(par):
        def item(r0):
            seg_ref[0, 0, r0:r0 + rc, :] = pc_scr[par, halo + r0:halo + r0 + rc, :].astype(BF16)
        return [functools.partial(item, r0) for r0 in chunks]

    n_groups = c // gw
    schedule = {
        SEG_GR: lambda par: gate_branch(par),
        SEG_XR: lambda par: lru_conv(par) + lru_gates(range(0, 1)),
        SEG_Q: lambda par: conv_silu(1, par),
        SEG_K: lambda par: conv_silu(2, par),
        SEG_V: lambda par: plain(par) + lru_gates(range(1, n_groups - 1)),
        SEG_O: lambda par: plain(par) + lru_gates(range(n_groups - 1, n_groups)),
        SEG_GA: lambda par: plain(par) + lru_scan(),
        SEG_GB: lambda par: plain(par),
    }

    @pl.when(m == 0)
    def _():
        carry_scr[...] = jnp.zeros_like(carry_scr)
        hlast_scr[...] = jnp.zeros_like(hlast_scr)
        for ci in range(cwb_scr.shape[0]):
            cwb_scr[ci] = _replicate_rows(jnp.concatenate([cw_ref[0, ci], cb_ref[0, ci]], axis=0))
        norm_and_gates()
        project(0)

    for j in range(N_SEG):
        @pl.when((m > 0) & (seg_done == j))
        def _(j=j):
            par = j % 2
            items = schedule[j](par)
            if j + 1 < N_SEG:
                interleave(1 - par, items)
            else:
                for item in items:
                    item()

                @pl.when(m < n_steps)
                def _():
                    norm_and_gates()
                    project(1 - par)


def _inproj(layer, h, norm_g, w_main, w_ifT, b_if, conv_w, conv_b, w_gate, b_gate, lam, *, tm=1024,
            rc=64, n_blk=2):
    b, s, d = h.shape
    n_groups = d // V7X_MXU_DIM
    halo = (CONV_W - 1) * V7X_SUBLANES
    assert tm % TOK_BLOCK == 0
    n_tiles = s // tm
    n_steps = n_tiles * N_SEG
    kern = functools.partial(_inproj_kernel, tm=tm, rc=rc, n_blk=n_blk, n_steps=n_steps)
    vmem = (2 * tm * d * 4 + 2 * d * d * 2 + 2 * tm * d * 2 + 2 * IFG_ROWS * tm * 4
            + tm * d * 2 + 2 * (tm + halo) * d * 4 + 4 * tm * d * 4 + tm * d * 2
            + 2 * n_groups * V7X_MXU_DIM * 2 * V7X_MXU_DIM * 2 + 3 * tm * d * 4)
    l = layer

    def proj_tile(m):
        return jnp.minimum(m // N_SEG, n_tiles - 1)

    def weight_block(m):
        seg = m % N_SEG
        return jnp.where(seg <= SEG_XR, SEG_XR - seg, seg)

    def done_tile(m):
        return jnp.maximum(m - 1, 0) // N_SEG

    def done_slot(m):
        return jnp.maximum(jnp.maximum(m - 1, 0) % N_SEG - FIRST_STORED_SEG, 0)

    return pl.pallas_call(
        kern,
        grid=(b, n_steps + 1),
        in_specs=[
            pl.BlockSpec((1, tm, d), lambda bi, m: (bi, proj_tile(m), 0)),
            pl.BlockSpec((1, 1, d), lambda bi, m: (l, 0, 0)),
            pl.BlockSpec((1, d, d), lambda bi, m: (l, 0, weight_block(m))),
            pl.BlockSpec((1, IFG_ROWS, d), lambda bi, m: (l, 0, 0)),
            pl.BlockSpec((1, IFG_ROWS, 1), lambda bi, m: (l, 0, 0)),
            pl.BlockSpec((1, 3, CONV_W, d), lambda bi, m: (l, 0, 0, 0)),
            pl.BlockSpec((1, 3, 1, d), lambda bi, m: (l, 0, 0, 0)),
            pl.BlockSpec((1, n_groups, V7X_MXU_DIM, 2 * V7X_MXU_DIM), lambda bi, m: (l, 0, 0, 0)),
            pl.BlockSpec((1, 2, 1, d), lambda bi, m: (l, 0, 0, 0)),
            pl.BlockSpec((1, 1, d), lambda bi, m: (l, 0, 0)),
        ],
        out_specs=[
            pl.BlockSpec((1, 1, tm, d), lambda bi, m: (done_slot(m), bi, done_tile(m), 0)),
            pl.BlockSpec((1, tm, d), lambda bi, m: (bi, done_tile(m), 0)),
            pl.BlockSpec((1, IFG_ROWS, tm), lambda bi, m: (bi, 0, proj_tile(m))),
        ],
        out_shape=[
            jax.ShapeDtypeStruct((N_SLOT, b, s, d), BF16),
            jax.ShapeDtypeStruct((b, s, d), BF16),
            jax.ShapeDtypeStruct((b, IFG_ROWS, s), F32),
        ],
        scratch_shapes=[
            pltpu.VMEM((tm, d), BF16),
            pltpu.VMEM((2, tm + halo, d), F32),
            pltpu.VMEM((tm, d), F32),
            pltpu.VMEM((tm, d), F32),
            pltpu.VMEM((tm, d), BF16),
            pltpu.VMEM((tm, 2 * V7X_MXU_DIM), F32),
            pltpu.VMEM((tm, d), F32),
            pltpu.VMEM((tm, d), F32),
            pltpu.VMEM((3, halo, d), F32),
            pltpu.VMEM((V7X_SUBLANES, d), F32),
            pltpu.VMEM((3, CONV_W + 1, V7X_SUBLANES, d), F32),
        ],
        compiler_params=_params(2, vmem),
        name=f"inproj{layer}",
    )(h, norm_g, w_main, w_ifT, b_if, conv_w, conv_b, w_gate, b_gate, lam)


def _lane_scan(x, lane, op, fill):
    n = x.shape[-1]
    k = 1
    while k < n:
        x = op(x, jnp.where(lane >= k, pltpu.roll(x, k, 1), fill))
        k *= 2
    return x


_DONE = object()


def _mlstm_kernel(q_ref, k_ref, v_ref, ifg_ref, y_ref, c_scr, m_scr):
    @pl.when(pl.program_id(1) == 0)
    def _():
        c_scr[...] = jnp.zeros_like(c_scr)
        m_scr[...] = jnp.zeros_like(m_scr)

    nb = y_ref.shape[0]
    gates = [_mlstm_gates(bb, ifg_ref, m_scr) for bb in range(nb)]
    heads = [_mlstm_head(bb, hd, gates[bb], q_ref, k_ref, v_ref, y_ref, c_scr)
             for hd in range(ML_HEADS) for bb in range(nb)]
    while heads:
        heads = [h for h in heads if next(h, _DONE) is not _DONE]


def _mlstm_gates(bb, ifg_ref, m_scr):
    L = ML_CHUNK
    rows = V7X_SUBLANES

    ifg = ifg_ref[bb]
    li = ifg[0:rows]
    lf = -_softplus(-ifg[rows:2 * rows])
    src = lax.broadcasted_iota(jnp.int32, (L, L), 0)
    dst = lax.broadcasted_iota(jnp.int32, (L, L), 1)
    not_after = (_time_of_row(src) <= _time_of_row(dst)).astype(F32)
    bcum = jnp.dot(lf, not_after, precision=lax.Precision.HIGHEST,
                   preferred_element_type=F32)
    m_st = m_scr[bb]
    g_tot = jnp.broadcast_to(bcum[:, L - 1:L], (rows, L))
    r1 = li - bcum
    u = g_tot - bcum + li
    m_next = jnp.maximum(g_tot + m_st, jnp.max(u, axis=-1, keepdims=True))
    decay = jnp.exp(g_tot + m_st - m_next)
    uexp = jnp.exp(u - m_next)
    m_scr[bb] = m_next

    stacked = jnp.concatenate(
        [bcum, m_st, jnp.zeros((L - 2 * rows, L), F32)], axis=0)
    cols = stacked.T
    return cols, r1, decay, uexp


def _mlstm_head(bb, hd, gates, q_ref, k_ref, v_ref, y_ref, c_scr):
    L = ML_CHUNK
    dh = c_scr.shape[2]
    rows = V7X_SUBLANES
    cols, r1, decay, uexp = gates
    tt = lax.broadcasted_iota(jnp.int32, (L, L), 0)
    ss = lax.broadcasted_iota(jnp.int32, (L, L), 1)
    causal = _time_of_row(ss) <= _time_of_row(tt)

    sl = slice(hd * dh, (hd + 1) * dh)
    qh = q_ref[0, bb, :, sl]
    kh = k_ref[0, bb, :, sl] * jnp.asarray(dh ** -0.5, BF16)
    vh = jnp.concatenate([v_ref[0, bb, :, sl], jnp.ones((L, V7X_LANES), BF16)], axis=1)
    b_col = cols[:, hd:hd + 1]
    a_inter = b_col + cols[:, rows + hd:rows + hd + 1]
    dec = decay[hd:hd + 1, 0:1]
    r1_row = r1[hd:hd + 1, :]

    r1_max = jnp.max(jnp.where(causal, r1_row, -jnp.inf), axis=-1, keepdims=True)
    m_t = jnp.maximum(a_inter, b_col + r1_max)
    qk = lax.dot_general(qh, kh, (((1,), (1,)), ((), ())), preferred_element_type=F32)
    c_old = c_scr[bb, hd]
    qc = jnp.dot(qh, c_old.astype(BF16), preferred_element_type=F32)
    yield
    sc = qk * jnp.exp(jnp.where(causal, (b_col - m_t) + r1_row, -jnp.inf))
    wic = jnp.exp(a_inter - m_t)
    enc = jnp.exp(-m_t)
    both = wic * qc + jnp.dot(sc.astype(BF16), vh, preferred_element_type=F32)
    wk_t = (kh.T.astype(F32) * uexp[hd:hd + 1, :]).astype(BF16)
    c_scr[bb, hd] = dec * c_old + jnp.dot(wk_t, vh, preferred_element_type=F32)
    yield
    num = both[:, :dh]
    den = both[:, dh:dh + 1]
    hh = num * (1.0 / jnp.maximum(jnp.abs(den), enc))
    y_ref[bb, :, sl] = hh.astype(BF16)


def _mlstm(layer, segs, ifg, *, nb=8):
    _, b, s, d = segs.shape
    L = ML_CHUNK
    dh = d // ML_HEADS

    def seg_spec(slot):
        return pl.BlockSpec((1, nb, L, d), lambda bi, ci: (slot, bi, ci, 0))

    vmem = nb * (2 * (3 * L * d * 2 + IFG_ROWS * L * 4 + L * d * 2)
                 + ML_HEADS * dh * (dh + V7X_LANES) * 4 + 16 * L * d * 4)
    return pl.pallas_call(
        _mlstm_kernel,
        grid=(b // nb, s // L),
        in_specs=[
            seg_spec(SLOT_Q), seg_spec(SLOT_K), seg_spec(SLOT_V),
            pl.BlockSpec((nb, IFG_ROWS, L), lambda bi, ci: (bi, 0, ci)),
        ],
        out_specs=pl.BlockSpec((nb, L, d), lambda bi, ci: (bi, ci, 0)),
        out_shape=jax.ShapeDtypeStruct((b, s, d), BF16),
        scratch_shapes=[
            pltpu.VMEM((nb, ML_HEADS, dh, dh + V7X_LANES), F32),
            pltpu.VMEM((nb, V7X_SUBLANES, L), F32),
        ],
        compiler_params=_params(2, vmem),
        name=f"mlstm{layer}",
    )(segs, segs, segs, ifg)


def _merge_kernel(ya_ref, hm_ref, og_ref, ga_ref, gb_ref, h_ref, ng_ref, wa_ref, wb_ref, wm_ref,
                  o_ref, yb_scr, *, n_blk):
    tm, d = yb_scr.shape
    dh = d // ML_HEADS
    bm = tm // n_blk
    for blk in range(n_blk):
        rows = slice(blk * bm, (blk + 1) * bm)
        for r0 in range(blk * bm, (blk + 1) * bm, TOK_BLOCK):
            for hd in range(ML_HEADS):
                sl = slice(hd * dh, (hd + 1) * dh)
                hh = hm_ref[0, r0:r0 + TOK_BLOCK, sl].astype(F32)
                mu = jnp.mean(hh, axis=-1, keepdims=True)
                cen = hh - mu
                var = jnp.mean(cen * cen, axis=-1, keepdims=True)
                hn = (cen * lax.rsqrt(var + EPS)) * ng_ref[0][:, sl]
                gate = jax.nn.sigmoid(og_ref[0, 0, r0:r0 + TOK_BLOCK, sl].astype(F32))
                yb_scr[r0:r0 + TOK_BLOCK, sl] = (gate * hn).astype(BF16)
        pa = jnp.dot(ya_ref[0, rows, :], wa_ref[0], preferred_element_type=F32)
        pb = jnp.dot(yb_scr[rows, :], wb_ref[0], preferred_element_type=F32)
        y = (jax.nn.sigmoid(ga_ref[0, 0, rows, :].astype(F32)) * pa
             + jax.nn.sigmoid(gb_ref[0, 0, rows, :].astype(F32)) * pb)
        o_ref[0, rows, :] = h_ref[0, rows, :] + jnp.dot(y.astype(BF16), wm_ref[0],
                                                        preferred_element_type=F32)


def _merge(layer, ya, hm, segs, h, ml_norm_g, wa, wb, wm, *, tm=1024, n_blk=2):
    b, s, d = h.shape
    l = layer

    def seg_spec(slot):
        return pl.BlockSpec((1, 1, tm, d), lambda bi, i: (slot, bi, i, 0))

    w_spec = pl.BlockSpec((1, d, d), lambda bi, i: (l, 0, 0))
    row_spec = pl.BlockSpec((1, tm, d), lambda bi, i: (bi, i, 0))
    vmem = 2 * (5 * tm * d * 2 + 2 * tm * d * 4 + 3 * d * d * 2) + tm * d * 2 + 4 * tm * d * 4
    return pl.pallas_call(
        functools.partial(_merge_kernel, n_blk=n_blk),
        grid=(b, s // tm),
        in_specs=[row_spec, row_spec, seg_spec(SLOT_O), seg_spec(SLOT_GA), seg_spec(SLOT_GB),
                  row_spec, pl.BlockSpec((1, 1, d), lambda bi, i: (l, 0, 0)),
                  w_spec, w_spec, w_spec],
        out_specs=row_spec,
        out_shape=jax.ShapeDtypeStruct((b, s, d), F32),
        scratch_shapes=[pltpu.VMEM((tm, d), BF16)],
        compiler_params=_params(2, vmem),
        name=f"merge{layer}",
    )(ya, hm, segs, segs, segs, h, ml_norm_g, wa, wb, wm)


def _xattn_kernel(h_ref, g_ref, wq_ref, kv_ref, wo_ref, o_ref, q_scr, att_scr, *, n_blk):
    tm, d = att_scr.shape
    dh = d // XA_HEADS
    bm = tm // n_blk
    scale = dh ** -0.5

    def head(rows, hd):
        sl = slice(hd * dh, (hd + 1) * dh)
        kh = kv_ref[0, 0, :, sl]
        vh = kv_ref[0, 0, :, d + hd * dh:d + (hd + 1) * dh]
        sc = lax.dot_general(q_scr[rows, sl], kh, (((1,), (1,)), ((), ())),
                             preferred_element_type=F32) * scale
        top = jnp.max(sc, axis=-1, keepdims=True)
        yield
        e = jnp.exp(sc - top)
        den = jnp.sum(e, axis=-1, keepdims=True)
        yield
        p = e * (1.0 / den)
        att_scr[rows, sl] = jnp.dot(p.astype(BF16), vh, preferred_element_type=F32).astype(BF16)

    for blk in range(n_blk):
        rows = slice(blk * bm, (blk + 1) * bm)
        xn = _rms_norm(h_ref[0, rows, :], g_ref[0]).astype(BF16)
        q_scr[rows, :] = jnp.dot(xn, wq_ref[0], preferred_element_type=F32).astype(BF16)
        heads = [head(rows, hd) for hd in range(XA_HEADS)]
        while heads:
            heads = [h for h in heads if next(h, _DONE) is not _DONE]
        o_ref[0, rows, :] = h_ref[0, rows, :] + jnp.dot(att_scr[rows, :], wo_ref[0],
                                                        preferred_element_type=F32)


def _xattn(layer, h, norm_g, wq, kv, wo, *, tm=1024, n_blk=2):
    b, s, d = h.shape
    m = kv.shape[2]
    l = layer
    row_spec = pl.BlockSpec((1, tm, d), lambda bi, i: (bi, i, 0))
    w_spec = pl.BlockSpec((1, d, d), lambda bi, i: (l, 0, 0))
    vmem = 2 * (2 * tm * d * 4 + 2 * d * d * 2 + m * 2 * d * 2) + 2 * tm * d * 2 + 6 * tm * d * 4
    return pl.pallas_call(
        functools.partial(_xattn_kernel, n_blk=n_blk),
        grid=(b, s // tm),
        in_specs=[
            row_spec,
            pl.BlockSpec((1, 1, d), lambda bi, i: (l, 0, 0)),
            w_spec,
            pl.BlockSpec((1, 1, m, 2 * d), lambda bi, i: (l, bi, 0, 0)),
            w_spec,
        ],
        out_specs=row_spec,
        out_shape=jax.ShapeDtypeStruct((b, s, d), F32),
        scratch_shapes=[pltpu.VMEM((tm, d), BF16),
                        pltpu.VMEM((tm, d), BF16)],
        compiler_params=_params(2, vmem),
        name=f"xattn{layer}",
    )(h, norm_g, wq, kv, wo)


def _ffn_kernel(x_ref, g_ref, wg_ref, wu_ref, cwg_ref, cwu_ref, cbg_ref, cbu_ref, wd_ref, fg_ref,
                o_ref, xn_scr, acc_scr, pc_scr, act_scr, carry_scr, cwb_scr, *, tm, rc, n_j, n_blk,
                final_norm):
    i = pl.program_id(1)
    j = pl.program_id(2)
    halo = (FFN_CONV_W - 1) * V7X_SUBLANES
    tf = act_scr.shape[1]

    @pl.when(j == 0)
    def _():
        xn_scr[...] = _rms_norm(x_ref[0], g_ref[0]).astype(BF16)
        acc_scr[...] = jnp.zeros_like(acc_scr)

    @pl.when(i == 0)
    def _():
        carry_scr[j] = jnp.zeros(carry_scr.shape[1:], F32)

    for half in range(2):
        pc_scr[half, 0:halo, :] = carry_scr[j, half]

    for half, (cw_ref, cb_ref) in enumerate(((cwg_ref, cbg_ref), (cwu_ref, cbu_ref))):
        cwb_scr[half] = _replicate_rows(jnp.concatenate([cw_ref[0], cb_ref[0]], axis=0))

    def conv_run(half, r0, lanes):
        base = halo + r0
        return _causal_conv_run(pc_scr[half, base - halo:base, lanes],
                                pc_scr[half, base:base + TOK_BLOCK, lanes],
                                cwb_scr[half, :, :, lanes])

    bm = tm // n_blk
    d = acc_scr.shape[1]
    pw = V7X_MXU_DIM

    def up_pieces(b):
        def piece(half, w_ref, n0):
            pc_scr[half, halo + b * bm:halo + (b + 1) * bm, n0:n0 + pw] = jnp.dot(
                xn_scr[b * bm:(b + 1) * bm, :], w_ref[0, :, n0:n0 + pw],
                preferred_element_type=F32)
        return [functools.partial(piece, half, w_ref, n0)
                for half, w_ref in enumerate((wg_ref, wu_ref)) for n0 in range(0, tf, pw)]

    def epilogue_items(b):
        def item(r0, lanes):
            gpre = conv_run(0, r0, lanes)
            upre = conv_run(1, r0, lanes)
            act_scr[r0:r0 + TOK_BLOCK, lanes] = (_gelu_tanh(gpre) * upre).astype(BF16)
        item_w = 2 * V7X_LANES
        return [functools.partial(item, r0, slice(l0, l0 + item_w))
                for r0 in range(b * bm, (b + 1) * bm, TOK_BLOCK)
                for l0 in range(0, tf, item_w)]

    def down_pieces(b):
        rows = slice(b * bm, (b + 1) * bm)

        def piece(n0):
            acc_scr[rows, n0:n0 + pw] += jnp.dot(act_scr[rows, :], wd_ref[0, :, n0:n0 + pw],
                                                 preferred_element_type=F32)
        return [functools.partial(piece, n0) for n0 in range(0, d, pw)]

    def interleave(major, minor):
        n, k = len(minor), len(major)
        for i, piece in enumerate(major):
            piece()
            for item in minor[i * n // k:(i + 1) * n // k]:
                item()

    interleave(up_pieces(0), [])
    for b in range(n_blk):
        matmuls = (up_pieces(b + 1) if b + 1 < n_blk else []) + (down_pieces(b - 1) if b else [])
        interleave(matmuls, epilogue_items(b))
    interleave(down_pieces(n_blk - 1), [])

    for half in range(2):
        carry_scr[j, half] = pc_scr[half, tm:tm + halo, :]

    @pl.when(j == n_j - 1)
    def _():
        oc = 8 * rc

        def out_body(r0):
            res = x_ref[0, pl.ds(r0, oc), :] + acc_scr[pl.ds(r0, oc), :]
            if final_norm:
                res = _rms_norm(res, fg_ref[...])
            o_ref[0, pl.ds(r0, oc), :] = res
        _row_chunks(tm, oc, out_body)


def _ffn(layer, h, norm_g, w_up, conv_w, conv_b, w_down, final_g, *, final_norm, tm=1024, rc=32,
         n_blk=2):
    b, s, d = h.shape
    tf = FFN_TILE
    n_j = w_up.shape[2] // (2 * tf)
    halo = (FFN_CONV_W - 1) * V7X_SUBLANES
    assert (tm // n_blk) % TOK_BLOCK == 0
    l = layer
    kern = functools.partial(_ffn_kernel, tm=tm, rc=rc, n_j=n_j, n_blk=n_blk,
                             final_norm=final_norm)
    row_spec = pl.BlockSpec((1, tm, d), lambda bi, i, j: (bi, i, 0))
    vmem = (4 * tm * d * 4 + 2 * (2 * d * tf * 2 + tf * d * 2) + tm * d * 2 + tm * d * 4
            + 2 * (tm + halo) * tf * 4 + tm * tf * 2 + n_j * 2 * halo * tf * 4 + 2 * tm * tf * 4)
    return pl.pallas_call(
        kern,
        grid=(b, s // tm, n_j),
        in_specs=[
            row_spec,
            pl.BlockSpec((1, 1, d), lambda bi, i, j: (l, 0, 0)),
            pl.BlockSpec((1, d, tf), lambda bi, i, j: (l, 0, j)),
            pl.BlockSpec((1, d, tf), lambda bi, i, j: (l, 0, n_j + j)),
            pl.BlockSpec((1, FFN_CONV_W, tf), lambda bi, i, j: (l, 0, j)),
            pl.BlockSpec((1, FFN_CONV_W, tf), lambda bi, i, j: (l, 0, n_j + j)),
            pl.BlockSpec((1, 1, tf), lambda bi, i, j: (l, 0, j)),
            pl.BlockSpec((1, 1, tf), lambda bi, i, j: (l, 0, n_j + j)),
            pl.BlockSpec((1, tf, d), lambda bi, i, j: (l, j, 0)),
            pl.BlockSpec((1, d), lambda bi, i, j: (0, 0)),
        ],
        out_specs=row_spec,
        out_shape=jax.ShapeDtypeStruct((b, s, d), F32),
        scratch_shapes=[
            pltpu.VMEM((tm, d), BF16),
            pltpu.VMEM((tm, d), F32),
            pltpu.VMEM((2, tm + halo, tf), F32),
            pltpu.VMEM((tm, tf), BF16),
            pltpu.VMEM((n_j, 2, halo, tf), F32),
            pltpu.VMEM((2, FFN_CONV_W + 1, V7X_SUBLANES, tf), F32),
        ],
        compiler_params=_params(3, vmem),
        name=f"ffn{layer}",
    )(h, norm_g, w_up, w_up, conv_w, conv_w, conv_b, conv_b, w_down, final_g)


def _block_diag_groups(w):
    depth, n_blocks, bs, _ = w.shape
    per = V7X_MXU_DIM // bs
    wg = w.reshape(depth, n_blocks // per, per, bs, bs)
    eye = jnp.eye(per, dtype=w.dtype)
    out = wg[:, :, :, :, None, :] * eye[None, None, :, None, :, None]
    return out.reshape(depth, n_blocks // per, per * bs, per * bs)


def kernel(x, mem, norm_mix_g, w_in, rnn_conv_w, rnn_conv_b, lru_wa, lru_ba, lru_wx, lru_bx, lru_lambda, ml_conv_w, ml_conv_b, ml_if_b, ml_norm_g, w_branch_a, w_branch_b, w_mix_out, norm_xa_g, xa_wq, xa_wkv, xa_wo, norm_ffn_g, ffn_w_up, ffn_conv_w, ffn_conv_b, ffn_w_down, mem_norm_g, final_norm_g):
    depth, d, _ = w_in.shape
    d_ml = ml_norm_g.shape[1]
    assert d == d_ml == lru_lambda.shape[1], "kernels assume D_RNN == D_ML == D_MODEL"
    assert ML_HEADS <= V7X_SUBLANES and x.shape[1] % ML_CHUNK == 0

    o_if = 6 * d
    o_ga = o_if + 2 * ML_HEADS

    w_main = jnp.concatenate([w_in[:, :, :o_if], w_in[:, :, o_ga:]], axis=-1).astype(BF16)
    w_if = jnp.swapaxes(w_in[:, :, o_if:o_ga], 1, 2)
    pad = jnp.zeros((depth, V7X_SUBLANES - ML_HEADS, d), w_in.dtype)
    w_ifT = jnp.concatenate([w_if[:, :ML_HEADS], pad, w_if[:, ML_HEADS:], pad], axis=1).astype(BF16)
    bpad = jnp.zeros((depth, V7X_SUBLANES - ML_HEADS), ml_if_b.dtype)
    b_if = jnp.concatenate([ml_if_b[:, :ML_HEADS], bpad, ml_if_b[:, ML_HEADS:], bpad], axis=1)[..., None]

    conv_w = jnp.stack([rnn_conv_w, ml_conv_w[:, :, :d], ml_conv_w[:, :, d:]], axis=1)
    conv_b = jnp.stack([rnn_conv_b, ml_conv_b[:, :d], ml_conv_b[:, d:]], axis=1)[:, :, None, :]
    w_gate = jnp.concatenate([_block_diag_groups(lru_wa), _block_diag_groups(lru_wx)], axis=-1).astype(BF16)
    b_gate = jnp.stack([lru_ba, lru_bx], axis=1)[:, :, None, :]

    def row(g):
        return g[:, None, :]

    wa, wb, wm = (w.astype(BF16) for w in (w_branch_a, w_branch_b, w_mix_out))
    wq, wkv, wo = (w.astype(BF16) for w in (xa_wq, xa_wkv, xa_wo))
    w_down = ffn_w_down.astype(BF16)
    w_up = ffn_w_up.astype(BF16)
    ffn_cb = row(ffn_conv_b)

    kv = _memkv(mem, mem_norm_g[None, :], wkv)
    h = _to_block_order(x)
    for l in range(depth):
        segs, ya, ifg = _inproj(l, h, row(norm_mix_g), w_main, w_ifT, b_if, conv_w, conv_b, w_gate,
                                b_gate, row(lru_lambda))
        hm = _mlstm(l, segs, ifg)
        h = _merge(l, ya, hm, segs, h, row(ml_norm_g), wa, wb, wm)
        h = _xattn(l, h, row(norm_xa_g), wq, kv, wo)
        h = _ffn(l, h, row(norm_ffn_g), w_up, ffn_conv_w, ffn_cb, w_down, final_norm_g[None, :],
                 final_norm=(l == depth - 1))
    return _from_block_order(h)
```
